```python
import jax
import jax.numpy as jnp
from jax import lax
import numpy as np

D_MODEL = 4096
BATCH = 1
SEQ = 8192
DEPTH = 4

GRID_W = 64
CTX_LEN = 256
N_MIXERS = 3
N_A_LAYERS = (DEPTH + 2) // 3
N_B_LAYERS = (DEPTH + 1) // 3
N_C_LAYERS = DEPTH // 3
HEAD_DIM = 128
N_HEADS = D_MODEL // HEAD_DIM
N_KV_HEADS = N_HEADS // 4
GROUP = N_HEADS // N_KV_HEADS
QKV_OUT = (N_HEADS + 2 * N_KV_HEADS) * HEAD_DIM
Q_BLOCK = 128
WINDOW = 128
ROPE_THETA = 10000.0
ROPE_FREQS = HEAD_DIM // 4
D_RNN = D_MODEL
RNN_BLOCK = 256
N_RNN_BLOCKS = D_RNN // RNN_BLOCK
CONV_W = 4
LRU_C = 8.0
N_EXPERTS = 16
CAPACITY_FACTOR = 2
D_EXPERT = 3 * D_MODEL // 16
MOD_RANK = D_MODEL // 8
EPS = 1e-6

kernel_name = 'hybrid_interleaved_diffusion_trunk'


def rms_norm(x, gain):
    xf = x.astype(jnp.float32)
    y = xf * lax.rsqrt(jnp.mean(xf * xf, axis=-1, keepdims=True) + EPS)
    return (y * gain.astype(jnp.float32)).astype(x.dtype)


def modulation(cond, w_down, w_up, bias):
    m = (jax.nn.silu(cond) @ w_down) @ w_up + bias
    return jnp.split(m[:, None, :], 6, axis=-1)


def modulate(x, gain, shift, scale):
    return rms_norm(x, gain) * (1 + scale) + shift


def axial_rope_tables(rows):
    row = jnp.repeat(jnp.arange(rows), GRID_W).astype(jnp.float32)
    col = jnp.tile(jnp.arange(GRID_W), rows).astype(jnp.float32)
    inv_freq = ROPE_THETA ** (-jnp.arange(ROPE_FREQS, dtype=jnp.float32) / ROPE_FREQS)
    ang = jnp.stack([row[:, None] * inv_freq, col[:, None] * inv_freq], axis=1)
    return jnp.cos(ang), jnp.sin(ang)


def apply_axial_rope(x, cos, sin):
    xr = x.astype(jnp.float32).reshape(x.shape[:-1] + (2, 2, ROPE_FREQS))
    x1, x2 = xr[..., 0, :], xr[..., 1, :]
    bshape = (1, cos.shape[0]) + (1,) * (x.ndim - 3) + (2, ROPE_FREQS)
    c, s = cos.reshape(bshape), sin.reshape(bshape)
    out = jnp.stack([x1 * c - x2 * s, x2 * c + x1 * s], axis=-2)
    return out.reshape(x.shape).astype(x.dtype)


def project_qkv(h, w_qkv):
    B, N, _ = h.shape
    qkv = h @ w_qkv
    nq, nk = N_HEADS * HEAD_DIM, N_KV_HEADS * HEAD_DIM
    q = qkv[..., :nq].reshape(B, N, N_KV_HEADS, GROUP, HEAD_DIM)
    k = qkv[..., nq:nq + nk].reshape(B, N, N_KV_HEADS, HEAD_DIM)
    v = qkv[..., nq + nk:].reshape(B, N, N_KV_HEADS, HEAD_DIM)
    return q, k, v


def gqa_softmax(q, k, v, mask, sink):
    s = jnp.einsum('bqkgd,bnkd->bkgqn', q, k).astype(jnp.float32) * (HEAD_DIM ** -0.5)
    if mask is not None:
        s = jnp.where(mask, s, -jnp.inf)
    if sink is not None:
        sk = jnp.broadcast_to(sink.astype(jnp.float32).reshape(N_KV_HEADS, GROUP, 1, 1), s.shape[:-1] + (1,))
        s = jnp.concatenate([s, sk], axis=-1)
    p = jax.nn.softmax(s, axis=-1)
    if sink is not None:
        p = p[..., :-1]
    return jnp.einsum('bkgqn,bnkd->bqkgd', p.astype(v.dtype), v)


def windowed_sink_attention(h_ctx, h_lat, w_qkv, w_o, sink, rope, need_ctx):
    B, S, _ = h_lat.shape
    qc, kc, vc = project_qkv(h_ctx, w_qkv)
    ql, kl, vl = project_qkv(h_lat, w_qkv)
    ql = apply_axial_rope(ql, *rope)
    kl = apply_axial_rope(kl, *rope)
    n_blocks = S // Q_BLOCK
    span = Q_BLOCK + 2 * WINDOW
    kp = jnp.pad(kl, ((0, 0), (WINDOW, WINDOW), (0, 0), (0, 0)))
    vp = jnp.pad(vl, ((0, 0), (WINDOW, WINDOW), (0, 0), (0, 0)))
    q_rel = jnp.arange(Q_BLOCK)[:, None]
    k_rel = jnp.arange(span)[None, :]
    in_window = (k_rel >= q_rel) & (k_rel <= q_rel + 2 * WINDOW)
    ctx_mask = jnp.ones((Q_BLOCK, kc.shape[1]), dtype=bool)

    def block(args):
        q_blk, start = args
        k_blk = lax.dynamic_slice_in_dim(kp, start, span, axis=1)
        v_blk = lax.dynamic_slice_in_dim(vp, start, span, axis=1)
        key_pos = start - WINDOW + jnp.arange(span)
        mask = in_window & ((key_pos >= 0) & (key_pos < S))[None, :]
        mask = jnp.concatenate([mask, ctx_mask], axis=1)
        return gqa_softmax(q_blk, jnp.concatenate([k_blk, kc], axis=1),
                           jnp.concatenate([v_blk, vc], axis=1), mask, sink)

    q_blocks = jnp.moveaxis(ql.reshape(B, n_blocks, Q_BLOCK, N_KV_HEADS, GROUP, HEAD_DIM), 1, 0)
    starts = jnp.arange(n_blocks) * Q_BLOCK
    o = lax.map(block, (q_blocks, starts))
    y_lat = jnp.moveaxis(o, 0, 1).reshape(B, S, N_HEADS * HEAD_DIM) @ w_o
    y_ctx = None
    if need_ctx:
        y_ctx = gqa_softmax(qc, kc, vc, None, sink).reshape(B, -1, N_HEADS * HEAD_DIM) @ w_o
    return y_ctx, y_lat


def axial_qknorm_attention(h_ctx, h_lat, w_qkv, w_o, q_gain, k_gain, rope, need_ctx):
    B, S, _ = h_lat.shape
    qc, kc, vc = project_qkv(h_ctx, w_qkv)
    ql, kl, vl = project_qkv(h_lat, w_qkv)
    qc, kc = rms_norm(qc, q_gain), rms_norm(kc, k_gain)
    ql = apply_axial_rope(rms_norm(ql, q_gain), *rope)
    kl = apply_axial_rope(rms_norm(kl, k_gain), *rope)
    k_all = jnp.concatenate([kc, kl], axis=1)
    v_all = jnp.concatenate([vc, vl], axis=1)
    n_blocks = S // Q_BLOCK
    q_blocks = jnp.moveaxis(ql.reshape(B, n_blocks, Q_BLOCK, N_KV_HEADS, GROUP, HEAD_DIM), 1, 0)
    o = lax.map(lambda q_blk: gqa_softmax(q_blk, k_all, v_all, None, None), q_blocks)
    y_lat = jnp.moveaxis(o, 0, 1).reshape(B, S, N_HEADS * HEAD_DIM) @ w_o
    y_ctx = None
    if need_ctx:
        y_ctx = gqa_softmax(qc, kc, vc, None, None).reshape(B, -1, N_HEADS * HEAD_DIM) @ w_o
    return y_ctx, y_lat


def centred_depthwise_conv(x, w, b):
    n = x.shape[1]
    left = CONV_W // 2
    xp = jnp.pad(x, ((0, 0), (left, CONV_W - 1 - left), (0, 0)))
    return sum(xp[:, k:k + n] * w[k] for k in range(CONV_W)) + b


def block_diag_linear(x, w, b):
    xb = x.reshape(x.shape[:-1] + (N_RNN_BLOCKS, RNN_BLOCK))
    return jnp.einsum('bnhi,hij->bnhj', xb, w).reshape(x.shape) + b


def rglru_coeffs(u, w_r, b_r, w_i, b_i, lam):
    uf = u.astype(jnp.float32)
    r = jax.nn.sigmoid(block_diag_linear(uf, w_r.astype(jnp.float32), b_r.astype(jnp.float32)))
    gi = jax.nn.sigmoid(block_diag_linear(uf, w_i.astype(jnp.float32), b_i.astype(jnp.float32)))
    log_a = -LRU_C * r * jax.nn.softplus(-lam.astype(jnp.float32))
    return jnp.exp(log_a), jnp.sqrt(-jnp.expm1(2.0 * log_a)) * (gi * uf)


def linear_scan(a, bx, h0, reverse):
    first = -1 if reverse else 0
    bx = bx.at[:, first].add(a[:, first] * h0)

    def combine(e1, e2):
        a1, b1 = e1
        a2, b2 = e2
        return a1 * a2, a2 * b1 + b2

    _, h = lax.associative_scan(combine, (a, bx), reverse=reverse, axis=1)
    return h


def rglru_mixer(h_ctx, h_lat, w_in, conv_w, conv_b, w_ra, b_ra, w_ix, b_ix, lam, w_out, need_ctx):
    def branches(h):
        z = h @ w_in
        return jax.nn.gelu(z[..., :D_RNN]), centred_depthwise_conv(z[..., D_RNN:], conv_w, conv_b)

    gate_ctx, u_ctx = branches(h_ctx)
    gate_lat, u_lat = branches(h_lat)
    h0 = jnp.zeros((h_lat.shape[0], D_RNN), jnp.float32)
    rec_ctx, rec_lat = 0.0, 0.0
    for d in range(2):
        rev = d == 1
        a_c, b_c = rglru_coeffs(u_ctx, w_ra[d], b_ra[d], w_ix[d], b_ix[d], lam[d])
        h_c = linear_scan(a_c, b_c, h0, rev)
        h_end = h_c[:, 0] if rev else h_c[:, -1]
        a_l, b_l = rglru_coeffs(u_lat, w_ra[d], b_ra[d], w_ix[d], b_ix[d], lam[d])
        rec_lat = rec_lat + linear_scan(a_l, b_l, h_end, rev)
        if need_ctx:
            rec_ctx = rec_ctx + h_c
    y_lat = (gate_lat * rec_lat.astype(gate_lat.dtype)) @ w_out
    y_ctx = None
    if need_ctx:
        y_ctx = (gate_ctx * rec_ctx.astype(gate_ctx.dtype)) @ w_out
    return y_ctx, y_lat


def expert_choice_ffn(h, w_router, w_gate, w_up, w_down):
    B, N, D = h.shape
    cap = CAPACITY_FACTOR * N // N_EXPERTS
    aff = jax.nn.softmax((h @ w_router).astype(jnp.float32), axis=-1)
    g, idx = lax.top_k(jnp.swapaxes(aff, 1, 2), cap)
    xs = jax.vmap(lambda hb, ib: hb[ib])(h, idx)
    a = jnp.einsum('becd,edf->becf', xs, w_gate)
    u = jnp.einsum('becd,edf->becf', xs, w_up)
    y = jnp.einsum('becf,efd->becd', jax.nn.silu(a) * u, w_down) * g[..., None].astype(h.dtype)
    return jax.vmap(lambda yb, ib: jnp.zeros((N, D), yb.dtype).at[ib.reshape(-1)].add(yb.reshape(-1, D)))(y, idx)


def setup_inputs(seed: int = 0) -> dict:
    key = jax.random.key(seed)
    ks = iter(jax.random.split(key, 40))

    def nrm(shape, std):
        return std * jax.random.normal(next(ks), shape, jnp.float32)

    def gain(shape):
        return 1.0 + nrm(shape, 0.05)

    a_pow = jax.random.uniform(next(ks), (N_C_LAYERS, 2, D_RNN), jnp.float32, 0.9, 0.999)
    a_base = a_pow ** (1.0 / LRU_C)
    c_lambda = jnp.log(a_base) - jnp.log1p(-a_base)
    hd_all = N_HEADS * HEAD_DIM
    inputs = {}
    inputs['x'] = nrm((BATCH, SEQ, D_MODEL), 1.0)
    inputs['c'] = nrm((BATCH, D_MODEL), 1.0)
    inputs['ctx'] = nrm((BATCH, CTX_LEN, D_MODEL), 1.0)
    inputs['c_ctx'] = nrm((D_MODEL,), 1.0)
    inputs['mod_down'] = nrm((DEPTH, D_MODEL, MOD_RANK), D_MODEL ** -0.5)
    inputs['mod_up'] = nrm((DEPTH, MOD_RANK, 6 * D_MODEL), 0.5 * MOD_RANK ** -0.5)
    inputs['mod_bias'] = nrm((DEPTH, 6 * D_MODEL), 0.02)
    inputs['norm_mix'] = gain((DEPTH, D_MODEL))
    inputs['norm_ffn'] = gain((DEPTH, D_MODEL))
    inputs['norm_final'] = gain((D_MODEL,))
    inputs['a_w_qkv'] = nrm((N_A_LAYERS, D_MODEL, QKV_OUT), D_MODEL ** -0.5)
    inputs['a_w_o'] = nrm((N_A_LAYERS, hd_all, D_MODEL), hd_all ** -0.5)
    inputs['a_sink'] = nrm((N_A_LAYERS, N_HEADS), 0.5)
    inputs['b_w_qkv'] = nrm((N_B_LAYERS, D_MODEL, QKV_OUT), D_MODEL ** -0.5)
    inputs['b_w_o'] = nrm((N_B_LAYERS, hd_all, D_MODEL), hd_all ** -0.5)
    inputs['b_q_norm'] = gain((N_B_LAYERS, HEAD_DIM))
    inputs['b_k_norm'] = gain((N_B_LAYERS, HEAD_DIM))
    inputs['c_w_in'] = nrm((N_C_LAYERS, D_MODEL, 2 * D_RNN), D_MODEL ** -0.5)
    inputs['c_conv_w'] = nrm((N_C_LAYERS, CONV_W, D_RNN), CONV_W ** -0.5)
    inputs['c_conv_b'] = nrm((N_C_LAYERS, D_RNN), 0.02)
    inputs['c_w_ra'] = nrm((N_C_LAYERS, 2, N_RNN_BLOCKS, RNN_BLOCK, RNN_BLOCK), RNN_BLOCK ** -0.5)
    inputs['c_b_ra'] = nrm((N_C_LAYERS, 2, D_RNN), 0.1)
    inputs['c_w_ix'] = nrm((N_C_LAYERS, 2, N_RNN_BLOCKS, RNN_BLOCK, RNN_BLOCK), RNN_BLOCK ** -0.5)
    inputs['c_b_ix'] = nrm((N_C_LAYERS, 2, D_RNN), 0.1)
    inputs['c_lambda'] = c_lambda
    inputs['c_w_out'] = nrm((N_C_LAYERS, D_RNN, D_MODEL), D_RNN ** -0.5)
    inputs['moe_router'] = nrm((DEPTH, D_MODEL, N_EXPERTS), D_MODEL ** -0.5)
    inputs['moe_w_gate'] = nrm((DEPTH, N_EXPERTS, D_MODEL, D_EXPERT), D_MODEL ** -0.5)
    inputs['moe_w_up'] = nrm((DEPTH, N_EXPERTS, D_MODEL, D_EXPERT), D_MODEL ** -0.5)
    inputs['moe_w_down'] = nrm((DEPTH, N_EXPERTS, D_EXPERT, D_MODEL), D_EXPERT ** -0.5)
    return inputs


def reference(x, c, ctx, c_ctx, mod_down, mod_up, mod_bias, norm_mix, norm_ffn, norm_final,
              a_w_qkv, a_w_o, a_sink, b_w_qkv, b_w_o, b_q_norm, b_k_norm,
              c_w_in, c_conv_w, c_conv_b, c_w_ra, c_b_ra, c_w_ix, c_b_ix, c_lambda, c_w_out,
              moe_router, moe_w_gate, moe_w_up, moe_w_down):
    S = x.shape[1]
    rows = S // GRID_W
    rope = axial_rope_tables(rows)
    x_lat, x_ctx = x, ctx
    for i in range(DEPTH):
        last = i == DEPTH - 1
        sh1, sc1, g1, sh2, sc2, g2 = modulation(c, mod_down[i], mod_up[i], mod_bias[i])
        csh1, csc1, cg1, csh2, csc2, cg2 = modulation(c_ctx[None, :], mod_down[i], mod_up[i], mod_bias[i])
        h_lat = modulate(x_lat, norm_mix[i], sh1, sc1)
        h_ctx = modulate(x_ctx, norm_mix[i], csh1, csc1)
        kind, j = i % N_MIXERS, i // N_MIXERS
        if kind == 0:
            y_ctx, y_lat = windowed_sink_attention(h_ctx, h_lat, a_w_qkv[j], a_w_o[j], a_sink[j], rope, not last)
        elif kind == 1:
            y_ctx, y_lat = axial_qknorm_attention(h_ctx, h_lat, b_w_qkv[j], b_w_o[j], b_q_norm[j], b_k_norm[j],
                                                  rope, not last)
        else:
            y_ctx, y_lat = rglru_mixer(h_ctx, h_lat, c_w_in[j], c_conv_w[j], c_conv_b[j], c_w_ra[j], c_b_ra[j],
                                       c_w_ix[j], c_b_ix[j], c_lambda[j], c_w_out[j], not last)
        x_lat = x_lat + g1 * y_lat
        x_lat = x_lat + g2 * expert_choice_ffn(modulate(x_lat, norm_ffn[i], sh2, sc2),
                                               moe_router[i], moe_w_gate[i], moe_w_up[i], moe_w_down[i])
        if not last:
            x_ctx = x_ctx + cg1 * y_ctx
            x_ctx = x_ctx + cg2 * expert_choice_ffn(modulate(x_ctx, norm_ffn[i], csh2, csc2),
                                                   moe_router[i], moe_w_gate[i], moe_w_up[i], moe_w_down[i])
    return rms_norm(x_lat, norm_final)
```

```python
import functools
import math

import jax
import jax.numpy as jnp
from jax import lax
from jax.experimental import pallas as pl
from jax.experimental.pallas import tpu as pltpu

F32 = jnp.float32
BF16 = jnp.bfloat16

HEAD_DIM = 128
GROUP = 4
WINDOW = 128
GRID_W = 64
ROPE_THETA = 10000.0
ROPE_FREQS = HEAD_DIM // 4
CAPACITY_FACTOR = 2
LRU_C = 8.0
EPS = 1e-6
NEG = -1e30

V7X_VMEM_BYTES = 64 * 1024 * 1024
VMEM_LIMIT = V7X_VMEM_BYTES - 8 * 1024 * 1024
LANES = 128


def _pick(n, candidates):
    for c in candidates:
        if c <= n and n % c == 0:
            return c
    return n


def _params(sem):
    return pltpu.CompilerParams(dimension_semantics=sem, vmem_limit_bytes=VMEM_LIMIT)


def _small_mm_kernel(a_ref, w_ref, b_ref, o_ref, *, silu):
    a = a_ref[...]
    if silu:
        a = a * jax.nn.sigmoid(a)
    acc = jnp.dot(a.astype(BF16), w_ref[...].astype(BF16), preferred_element_type=F32)
    o_ref[...] = acc + b_ref[...]


def _small_mm(a, w, bias, *, silu):
    m, k = a.shape
    n = w.shape[1]
    tn = _pick(n, (2048, 1024, 512, 256, 128))
    return pl.pallas_call(
        functools.partial(_small_mm_kernel, silu=silu),
        out_shape=jax.ShapeDtypeStruct((m, n), F32),
        grid=(n // tn,),
        in_specs=[pl.BlockSpec((m, k), lambda j: (0, 0)),
                  pl.BlockSpec((k, tn), lambda j: (0, j)),
                  pl.BlockSpec((1, tn), lambda j: (0, j))],
        out_specs=pl.BlockSpec((m, tn), lambda j: (0, j)),
        compiler_params=_params(("arbitrary",)),
        name="small_mm",
    )(a, w, bias)


ROW_CHUNK = 64


def _modulated_rows(x_ref, gain_ref, sh_ref, sc_ref, row0, n_ctx, emit):
    tm = x_ref.shape[0]

    def body(r, carry):
        r0 = pl.multiple_of(r * ROW_CHUNK, ROW_CHUNK)
        x = x_ref[pl.ds(r0, ROW_CHUNK), :]
        ms = jnp.mean(x * x, axis=-1, keepdims=True)
        y = (x * lax.rsqrt(ms + EPS)) * gain_ref[...]
        rows = row0 + r0 + lax.broadcasted_iota(jnp.int32, (ROW_CHUNK, 1), 0)
        is_ctx = rows < n_ctx
        sc = jnp.where(is_ctx, sc_ref[1:2, :], sc_ref[0:1, :])
        sh = jnp.where(is_ctx, sh_ref[1:2, :], sh_ref[0:1, :])
        emit(r0, y * (1.0 + sc) + sh)
        return carry

    lax.fori_loop(0, tm // ROW_CHUNK, body, 0)


def _rope(a, cos, s1, s2):
    return a * cos + pltpu.roll(a, HEAD_DIM - ROPE_FREQS, 1) * s1 + pltpu.roll(a, ROPE_FREQS, 1) * s2


def _norm_mm_kernel(*refs, n_ctx, tm, tn, mode, n_rope_tiles):
    if mode in ("rope", "qknorm_rope"):
        if mode == "qknorm_rope":
            x_ref, gain_ref, sh_ref, sc_ref, w_ref, cos_ref, s1_ref, s2_ref, hg_ref, o_ref, h_scr = refs
        else:
            x_ref, gain_ref, sh_ref, sc_ref, w_ref, cos_ref, s1_ref, s2_ref, o_ref, h_scr = refs
    else:
        x_ref, gain_ref, sh_ref, sc_ref, w_ref, o_ref, h_scr = refs
    i = pl.program_id(0)
    j = pl.program_id(1)

    @pl.when(j == 0)
    def _():
        def emit(r0, h):
            h_scr[pl.ds(r0, ROW_CHUNK), :] = h.astype(BF16)

        _modulated_rows(x_ref, gain_ref, sh_ref, sc_ref, i * tm, n_ctx, emit)

    acc = jnp.dot(h_scr[...], w_ref[...], preferred_element_type=F32)

    if mode == "none":
        o_ref[...] = acc.astype(o_ref.dtype)
    elif mode == "gelu":
        o_ref[...] = jax.nn.gelu(acc).astype(o_ref.dtype)
    else:
        @pl.when(j < n_rope_tiles)
        def _():
            for s in range(tn // HEAD_DIM):
                a = acc[:, s * HEAD_DIM:(s + 1) * HEAD_DIM]
                if mode == "qknorm_rope":
                    ms = jnp.mean(a * a, axis=-1, keepdims=True)
                    a = (a * lax.rsqrt(ms + EPS)) * hg_ref[:, s * HEAD_DIM:(s + 1) * HEAD_DIM]
                a = _rope(a, cos_ref[...], s1_ref[...], s2_ref[...])
                o_ref[:, s * HEAD_DIM:(s + 1) * HEAD_DIM] = a.astype(o_ref.dtype)

        @pl.when(j >= n_rope_tiles)
        def _():
            o_ref[...] = acc.astype(o_ref.dtype)


def _norm_mm(xs, gain, sh, sc, w, *, n_ctx, mode, out_dtype, rope=None, head_gain=None, n_rope_cols=0):
    nt, d = xs.shape
    n = w.shape[1]
    tm = _pick(nt, (768, 512, 256))
    tn = _pick(math.gcd(n, n_rope_cols), (512, 256, 128))
    in_specs = [pl.BlockSpec((tm, d), lambda i, j: (i, 0)),
                pl.BlockSpec((1, d), lambda i, j: (0, 0)),
                pl.BlockSpec((2, d), lambda i, j: (0, 0)),
                pl.BlockSpec((2, d), lambda i, j: (0, 0)),
                pl.BlockSpec((d, tn), lambda i, j: (0, j))]
    args = [xs, gain, sh, sc, w]
    if mode in ("rope", "qknorm_rope"):
        assert n_rope_cols % tn == 0
        for t in rope:
            in_specs.append(pl.BlockSpec((tm, HEAD_DIM), lambda i, j: (i, 0)))
            args.append(t)
        if mode == "qknorm_rope":
            in_specs.append(pl.BlockSpec((1, tn), lambda i, j: (0, j)))
            args.append(head_gain)
    return pl.pallas_call(
        functools.partial(_norm_mm_kernel, n_ctx=n_ctx, tm=tm, tn=tn, mode=mode,
                          n_rope_tiles=n_rope_cols // tn),
        out_shape=jax.ShapeDtypeStruct((nt, n), out_dtype),
        grid=(nt // tm, n // tn),
        in_specs=in_specs,
        out_specs=pl.BlockSpec((tm, tn), lambda i, j: (i, j)),
        scratch_shapes=[pltpu.VMEM((tm, d), BF16)],
        compiler_params=_params(("arbitrary", "arbitrary")),
        name="norm_mm_" + mode,
    )(*args)


def _resid_mm_kernel(a_ref, w_ref, r_ref, g_ref, o_ref, *, n_ctx, tm):
    i = pl.program_id(0)
    acc = jnp.dot(a_ref[...], w_ref[...], preferred_element_type=F32)
    rows = i * tm + lax.broadcasted_iota(jnp.int32, (tm, 1), 0)
    g = jnp.where(rows < n_ctx, g_ref[1:2, :], g_ref[0:1, :])
    o_ref[...] = r_ref[...] + g * acc


def _resid_mm(a, w, resid, gate, *, n_ctx):
    nt, k = a.shape
    n = w.shape[1]
    tm = _pick(nt, (1056, 768, 640, 512, 256))
    tn = _pick(n, (512, 256, 128))
    return pl.pallas_call(
        functools.partial(_resid_mm_kernel, n_ctx=n_ctx, tm=tm),
        out_shape=jax.ShapeDtypeStruct((nt, n), F32),
        grid=(nt // tm, n // tn),
        in_specs=[pl.BlockSpec((tm, k), lambda i, j: (i, 0)),
                  pl.BlockSpec((k, tn), lambda i, j: (0, j)),
                  pl.BlockSpec((tm, tn), lambda i, j: (i, j)),
                  pl.BlockSpec((2, tn), lambda i, j: (0, j))],
        out_specs=pl.BlockSpec((tm, tn), lambda i, j: (i, j)),
        compiler_params=_params(("arbitrary", "arbitrary")),
        name="resid_mm",
    )(a, w, resid, gate)


def _attn_win_kernel(sink_ref, q_ref, kc_ref, vc_ref, k0_ref, k1_ref, k2_ref, v0_ref, v1_ref, v2_ref,
                     o_ref, *, tq, n_tiles, n_ctx_tiles, scale):
    h = pl.program_id(0)
    i = pl.program_id(1)
    q4 = jnp.concatenate([q_ref[:, g * HEAD_DIM:(g + 1) * HEAD_DIM] for g in range(GROUP)], axis=0)
    k_all = jnp.concatenate([kc_ref[...], k0_ref[...], k1_ref[...], k2_ref[...]], axis=0)
    v_all = jnp.concatenate([vc_ref[...], v0_ref[...], v1_ref[...], v2_ref[...]], axis=0)
    s = lax.dot_general(q4, k_all, (((1,), (1,)), ((), ())), preferred_element_type=F32) * scale

    d = lax.broadcasted_iota(jnp.int32, (tq, tq), 1) - lax.broadcasted_iota(jnp.int32, (tq, tq), 0)
    big = 4 * tq
    is_lat = i >= n_ctx_tiles
    lo0 = jnp.where(i >= n_ctx_tiles + 1, tq - WINDOW, big)
    w1 = jnp.where(is_lat, WINDOW, -1)
    hi2 = jnp.where(is_lat & (i <= n_tiles - 2), WINDOW - tq, -big)
    n_ctx = kc_ref.shape[0]
    mask = jnp.concatenate([jnp.ones((tq, n_ctx), jnp.bool_), d >= lo0, jnp.abs(d) <= w1, d <= hi2], axis=1)
    mask = jnp.concatenate([mask] * GROUP, axis=0)
    s = jnp.where(mask, s, NEG)

    sk = jnp.concatenate([jnp.full((tq, 1), sink_ref[h * GROUP + g], F32) for g in range(GROUP)], axis=0)
    m = jnp.maximum(jnp.max(s, axis=1, keepdims=True), sk)
    p = jnp.exp(s - m)
    l = jnp.sum(p, axis=1, keepdims=True) + jnp.exp(sk - m)
    o = jnp.dot(p.astype(BF16), v_all, preferred_element_type=F32) / l
    for g in range(GROUP):
        o_ref[:, g * HEAD_DIM:(g + 1) * HEAD_DIM] = o[g * tq:(g + 1) * tq].astype(o_ref.dtype)


def _attn_win(qkv, sink, *, n_ctx, n_heads, n_kv):
    nt = qkv.shape[0]
    tq = n_ctx
    assert tq % WINDOW == 0 and tq >= 2 * WINDOW and nt % tq == 0
    n_tiles = nt // tq
    n_ctx_tiles = n_ctx // tq
    kcol, vcol = n_heads, n_heads + n_kv

    def win(b, col):
        return pl.BlockSpec((tq, HEAD_DIM),
                            lambda h, i: (jnp.clip(i - 1 + b, 0, n_tiles - 1), col + h))

    in_specs = [pl.BlockSpec(memory_space=pltpu.SMEM),
                pl.BlockSpec((tq, GROUP * HEAD_DIM), lambda h, i: (i, h)),
                pl.BlockSpec((n_ctx, HEAD_DIM), lambda h, i: (0, kcol + h)),
                pl.BlockSpec((n_ctx, HEAD_DIM), lambda h, i: (0, vcol + h)),
                win(0, kcol), win(1, kcol), win(2, kcol),
                win(0, vcol), win(1, vcol), win(2, vcol)]
    return pl.pallas_call(
        functools.partial(_attn_win_kernel, tq=tq, n_tiles=n_tiles, n_ctx_tiles=n_ctx_tiles,
                          scale=HEAD_DIM ** -0.5),
        out_shape=jax.ShapeDtypeStruct((nt, n_heads * HEAD_DIM), BF16),
        grid=(n_kv, n_tiles),
        in_specs=in_specs,
        out_specs=pl.BlockSpec((tq, GROUP * HEAD_DIM), lambda h, i: (i, h)),
        compiler_params=_params(("arbitrary", "arbitrary")),
        name="attn_win",
    )(sink, qkv, qkv, qkv, qkv, qkv, qkv, qkv, qkv, qkv)


def _attn_full_kernel(q_ref, k_ref, v_ref, o_ref, m_scr, l_scr, acc_scr, *, tq, tk, n_ctx, scale):
    i = pl.program_id(1)
    nt = k_ref.shape[0]
    q4 = jnp.concatenate([q_ref[:, g * HEAD_DIM:(g + 1) * HEAD_DIM] for g in range(GROUP)], axis=0)

    def run(n_chunks, ctx_only):
        m_scr[...] = jnp.full(m_scr.shape, NEG, F32)
        l_scr[...] = jnp.zeros(l_scr.shape, F32)
        acc_scr[...] = jnp.zeros(acc_scr.shape, F32)

        def body(c, carry):
            off = pl.multiple_of(c * tk, tk)
            kc = k_ref[pl.ds(off, tk), :]
            vc = v_ref[pl.ds(off, tk), :]
            s = lax.dot_general(q4, kc, (((1,), (1,)), ((), ())), preferred_element_type=F32) * scale
            if ctx_only:
                col = off + lax.broadcasted_iota(jnp.int32, (1, tk), 1)
                s = jnp.where(col < n_ctx, s, NEG)
            m_prev = m_scr[...]
            m_new = jnp.maximum(m_prev, jnp.max(s, axis=1, keepdims=True))
            alpha = jnp.exp(m_prev - m_new)
            p = jnp.exp(s - m_new)
            l_scr[...] = alpha * l_scr[...] + jnp.sum(p, axis=1, keepdims=True)
            acc_scr[...] = alpha * acc_scr[...] + jnp.dot(p.astype(BF16), vc, preferred_element_type=F32)
            m_scr[...] = m_new
            return carry

        lax.fori_loop(0, n_chunks, body, 0)
        o = acc_scr[...] / l_scr[...]
        for g in range(GROUP):
            o_ref[:, g * HEAD_DIM:(g + 1) * HEAD_DIM] = o[g * tq:(g + 1) * tq].astype(o_ref.dtype)

    n_ctx_tiles = n_ctx // tq

    @pl.when(i < n_ctx_tiles)
    def _():
        run(-(-n_ctx // tk), True)

    @pl.when(i >= n_ctx_tiles)
    def _():
        run(nt // tk, False)


def _attn_full(qkv, *, n_ctx, n_heads, n_kv):
    nt = qkv.shape[0]
    tq = _pick(n_ctx, (256, 128))
    tk = _pick(nt, (768, 512, 256, 128))
    assert nt % tq == 0 and n_ctx % tq == 0
    kcol, vcol = n_heads, n_heads + n_kv
    return pl.pallas_call(
        functools.partial(_attn_full_kernel, tq=tq, tk=tk, n_ctx=n_ctx, scale=HEAD_DIM ** -0.5),
        out_shape=jax.ShapeDtypeStruct((nt, n_heads * HEAD_DIM), BF16),
        grid=(n_kv, nt // tq),
        in_specs=[pl.BlockSpec((tq, GROUP * HEAD_DIM), lambda h, i: (i, h)),
                  pl.BlockSpec((nt, HEAD_DIM), lambda h, i: (0, kcol + h)),
                  pl.BlockSpec((nt, HEAD_DIM), lambda h, i: (0, vcol + h))],
        out_specs=pl.BlockSpec((tq, GROUP * HEAD_DIM), lambda h, i: (i, h)),
        scratch_shapes=[pltpu.VMEM((GROUP * tq, 1), F32), pltpu.VMEM((GROUP * tq, 1), F32),
                        pltpu.VMEM((GROUP * tq, HEAD_DIM), F32)],
        compiler_params=_params(("arbitrary", "arbitrary")),
        name="attn_full",
    )(qkv, qkv, qkv)


def _rglru_kernel(u_ref, gate_ref, cw_ref, cb_ref, wr_ref, wi_ref, br_ref, bi_ref, lam_ref, o_ref, rec_scr,
                  *, tc, n_chunks, n_ctx_chunks):
    nt, ch = u_ref.shape
    row = lax.broadcasted_iota(jnp.int32, (tc, 1), 0)

    def conv_chunk(c):
        t0 = pl.multiple_of(c * tc, tc)
        x0 = u_ref[pl.ds(t0, tc), :]
        seg_start = (c == 0) | (c == n_ctx_chunks)
        seg_end = (c == n_ctx_chunks - 1) | (c == n_chunks - 1)
        p0 = pl.multiple_of(jnp.maximum(t0 - 8, 0), 8)
        n0 = pl.multiple_of(jnp.minimum(t0 + tc, nt - 8), 8)
        prev = u_ref[pl.ds(p0, 8), :] * jnp.where(seg_start, 0.0, 1.0)
        nxt = u_ref[pl.ds(n0, 8), :] * jnp.where(seg_end, 0.0, 1.0)
        xm1 = jnp.where(row == 0, prev[7:8, :], pltpu.roll(x0, 1, 0))
        xm2 = jnp.where(row == 0, prev[6:7, :], jnp.where(row == 1, prev[7:8, :], pltpu.roll(x0, 2, 0)))
        xp1 = jnp.where(row == tc - 1, nxt[0:1, :], pltpu.roll(x0, tc - 1, 0))
        uc = cw_ref[0:1, :] * xm2 + cw_ref[1:2, :] * xm1 + cw_ref[2:3, :] * x0 + cw_ref[3:4, :] * xp1
        return t0, uc + cb_ref[...]

    def coeffs(uc, d):
        ub = uc.astype(BF16)
        r = jax.nn.sigmoid(jnp.dot(ub, wr_ref[d, 0], preferred_element_type=F32) + br_ref[d:d + 1, :])
        gi = jax.nn.sigmoid(jnp.dot(ub, wi_ref[d, 0], preferred_element_type=F32) + bi_ref[d:d + 1, :])
        x = -lam_ref[d:d + 1, :]
        softplus = jnp.maximum(x, 0.0) + jnp.log1p(jnp.exp(-jnp.abs(x)))
        log_a = (-LRU_C * r) * softplus
        a = jnp.exp(log_a)
        one_minus_a2 = -jnp.tanh(log_a) * (a * a + 1.0)
        return a, jnp.sqrt(one_minus_a2) * (gi * uc)

    def scan_chunk(a, b, h0, reverse):
        first = tc - 1 if reverse else 0
        b = jnp.where(row == first, b + a * h0, b)
        s = 1
        while s < tc:
            if reverse:
                keep = row < tc - s
                a_sh, b_sh = pltpu.roll(a, tc - s, 0), pltpu.roll(b, tc - s, 0)
            else:
                keep = row >= s
                a_sh, b_sh = pltpu.roll(a, s, 0), pltpu.roll(b, s, 0)
            b = a * jnp.where(keep, b_sh, 0.0) + b
            a = a * jnp.where(keep, a_sh, 1.0)
            s *= 2
        return b

    def fwd(c, h):
        t0, uc = conv_chunk(c)
        a, b = coeffs(uc, 0)
        hh = scan_chunk(a, b, h, False)
        rec_scr[pl.ds(t0, tc), :] = hh
        return hh[tc - 1:tc, :]

    def bwd(j, h):
        c = jnp.where(j < n_ctx_chunks, n_ctx_chunks - 1 - j, n_chunks - 1 - (j - n_ctx_chunks))
        t0, uc = conv_chunk(c)
        a, b = coeffs(uc, 1)
        hh = scan_chunk(a, b, h, True)
        rec = rec_scr[pl.ds(t0, tc), :] + hh
        o_ref[pl.ds(t0, tc), :] = (gate_ref[pl.ds(t0, tc), :].astype(F32) * rec).astype(o_ref.dtype)
        return hh[0:1, :]

    zero = jnp.zeros((1, ch), F32)
    lax.fori_loop(0, n_chunks, fwd, zero)
    lax.fori_loop(0, n_chunks, bwd, zero)


def _rglru(u, gate, conv_w, conv_b, w_ra, w_ix, b_ra, b_ix, lam, *, n_ctx):
    nt, d = u.shape
    nb, ch = w_ra.shape[1], w_ra.shape[2]
    tc = _pick(n_ctx, (256, 128))
    assert n_ctx % tc == 0 and nt % tc == 0
    col = lambda h: (0, h)
    return pl.pallas_call(
        functools.partial(_rglru_kernel, tc=tc, n_chunks=nt // tc, n_ctx_chunks=n_ctx // tc),
        out_shape=jax.ShapeDtypeStruct((nt, d), BF16),
        grid=(nb,),
        in_specs=[pl.BlockSpec((nt, ch), col),
                  pl.BlockSpec((nt, ch), col),
                  pl.BlockSpec((conv_w.shape[0], ch), col),
                  pl.BlockSpec((1, ch), col),
                  pl.BlockSpec((2, 1, ch, ch), lambda h: (0, h, 0, 0)),
                  pl.BlockSpec((2, 1, ch, ch), lambda h: (0, h, 0, 0)),
                  pl.BlockSpec((2, ch), col),
                  pl.BlockSpec((2, ch), col),
                  pl.BlockSpec((2, ch), col)],
        out_specs=pl.BlockSpec((nt, ch), col),
        scratch_shapes=[pltpu.VMEM((nt, ch), F32)],
        compiler_params=_params(("arbitrary",)),
        name="rglru",
    )(u, gate, conv_w, conv_b, w_ra, w_ix, b_ra, b_ix, lam)


def _ffn_pre_kernel(x_ref, gain_ref, sh_ref, sc_ref, wr_ref, hp_ref, aff_ref, *, n_ctx, tm, n_experts):
    i = pl.program_id(0)
    half = x_ref.shape[1] // 2

    def emit(r0, h):
        hb = h.astype(BF16)
        bits = pltpu.bitcast(hb.astype(F32), jnp.uint32)
        hp_ref[pl.ds(r0, ROW_CHUNK), :] = (bits[:, :half] & jnp.uint32(0xFFFF0000)) | (bits[:, half:] >> 16)
        logits = jnp.dot(hb, wr_ref[...], preferred_element_type=F32)
        lane = lax.broadcasted_iota(jnp.int32, logits.shape, 1)
        logits = jnp.where(lane < n_experts, logits, NEG)
        e = jnp.exp(logits - jnp.max(logits, axis=-1, keepdims=True))
        aff_ref[pl.ds(r0, ROW_CHUNK), :] = e / jnp.sum(e, axis=-1, keepdims=True)

    _modulated_rows(x_ref, gain_ref, sh_ref, sc_ref, i * tm, n_ctx, emit)


def _ffn_pre(xs, gain, sh, sc, w_router_pad, *, n_ctx, n_experts):
    nt, d = xs.shape
    tm = _pick(nt, (768, 512, 256))
    return pl.pallas_call(
        functools.partial(_ffn_pre_kernel, n_ctx=n_ctx, tm=tm, n_experts=n_experts),
        out_shape=(jax.ShapeDtypeStruct((nt, d // 2), jnp.uint32),
                   jax.ShapeDtypeStruct((nt, LANES), F32)),
        grid=(nt // tm,),
        in_specs=[pl.BlockSpec((tm, d), lambda i: (i, 0)),
                  pl.BlockSpec((1, d), lambda i: (0, 0)),
                  pl.BlockSpec((2, d), lambda i: (0, 0)),
                  pl.BlockSpec((2, d), lambda i: (0, 0)),
                  pl.BlockSpec((d, LANES), lambda i: (0, 0))],
        out_specs=(pl.BlockSpec((tm, d // 2), lambda i: (i, 0)),
                   pl.BlockSpec((tm, LANES), lambda i: (i, 0))),
        compiler_params=_params(("arbitrary",)),
        name="ffn_pre",
    )(xs, gain, sh, sc, w_router_pad)


def _moe_kernel(idx_ref, hp_hbm, g_ref, wg_ref, wu_ref, wd_ref, o_ref, land, xb, act, sem, *, nf, tf):
    e = pl.program_id(0)
    s = pl.program_id(1)
    n_experts = pl.num_programs(0)
    rows, half = land.shape

    def gather_rows(expert):
        def body(p, carry):
            src = idx_ref[expert, p]
            pltpu.make_async_copy(hp_hbm.at[pl.ds(src, 1), :], land.at[pl.ds(p, 1), :], sem).start()
            return carry

        lax.fori_loop(0, rows, body, 0)

    @pl.when((e == 0) & (s == 0))
    def _():
        gather_rows(0)

    @pl.when(s == 0)
    def _():
        pltpu.make_async_copy(hp_hbm.at[pl.ds(0, rows), :], land, sem).wait()
        w = land[...]
        xb[:, :half] = pltpu.bitcast(w & jnp.uint32(0xFFFF0000), F32).astype(BF16)
        xb[:, half:] = pltpu.bitcast(w << 16, F32).astype(BF16)

        @pl.when(e + 1 < n_experts)
        def _():
            gather_rows(e + 1)

    @pl.when(s < nf)
    def _():
        x = xb[...]
        a = jnp.dot(x, wg_ref[0].astype(BF16), preferred_element_type=F32)
        u = jnp.dot(x, wu_ref[0].astype(BF16), preferred_element_type=F32)
        act[s] = (a * jax.nn.sigmoid(a) * u).astype(BF16)

    @pl.when(s >= nf)
    def _():
        acc = jnp.dot(act[0], wd_ref[0, 0:tf, :].astype(BF16), preferred_element_type=F32)
        for f in range(1, nf):
            acc = acc + jnp.dot(act[f], wd_ref[0, f * tf:(f + 1) * tf, :].astype(BF16),
                                preferred_element_type=F32)
        o_ref[0] = acc * g_ref[0]


def _moe(idx, hp, gates, w_gate, w_up, w_down):
    n_experts, rows = idx.shape
    half = hp.shape[1]
    d = 2 * half
    dexp = w_gate.shape[2]
    tf = _pick(dexp, (256, 128))
    td = _pick(d, (512, 256, 128))
    nf, nd = dexp // tf, d // td
    grid_spec = pltpu.PrefetchScalarGridSpec(
        num_scalar_prefetch=1,
        grid=(n_experts, nf + nd),
        in_specs=[pl.BlockSpec(memory_space=pl.ANY),
                  pl.BlockSpec((1, rows, 1), lambda e, s, idx: (e, 0, 0)),
                  pl.BlockSpec((1, d, tf), lambda e, s, idx: (e, 0, jnp.minimum(s, nf - 1))),
                  pl.BlockSpec((1, d, tf), lambda e, s, idx: (e, 0, jnp.minimum(s, nf - 1))),
                  pl.BlockSpec((1, dexp, td), lambda e, s, idx: (e, 0, jnp.maximum(s - nf, 0)))],
        out_specs=pl.BlockSpec((1, rows, td), lambda e, s, idx: (e, 0, jnp.maximum(s - nf, 0))),
        scratch_shapes=[pltpu.VMEM((rows, half), jnp.uint32),
                        pltpu.VMEM((rows, d), BF16),
                        pltpu.VMEM((nf, rows, tf), BF16),
                        pltpu.SemaphoreType.DMA(())],
    )
    return pl.pallas_call(
        functools.partial(_moe_kernel, nf=nf, tf=tf),
        out_shape=jax.ShapeDtypeStruct((n_experts, rows, d), F32),
        grid_spec=grid_spec,
        compiler_params=_params(("arbitrary", "arbitrary")),
        name="moe_experts",
    )(idx, hp, gates, w_gate, w_up, w_down)


def _final_norm_kernel(x_ref, gain_ref, o_ref):
    x = x_ref[...]
    ms = jnp.mean(x * x, axis=-1, keepdims=True)
    o_ref[...] = (x * lax.rsqrt(ms + EPS)) * gain_ref[...]


def _final_norm(xs, gain, *, n_ctx):
    nt, d = xs.shape
    tm = _pick(n_ctx, (256, 128))
    off = n_ctx // tm
    return pl.pallas_call(
        _final_norm_kernel,
        out_shape=jax.ShapeDtypeStruct((nt - n_ctx, d), F32),
        grid=((nt - n_ctx) // tm,),
        in_specs=[pl.BlockSpec((tm, d), lambda i: (i + off, 0)),
                  pl.BlockSpec((1, d), lambda i: (0, 0))],
        out_specs=pl.BlockSpec((tm, d), lambda i: (i, 0)),
        compiler_params=_params(("arbitrary",)),
        name="final_norm",
    )(xs, gain)


def _rope_tables(seq, n_ctx):
    rows = seq // GRID_W
    row = jnp.repeat(jnp.arange(rows), GRID_W).astype(F32)
    col = jnp.tile(jnp.arange(GRID_W), rows).astype(F32)
    inv_freq = ROPE_THETA ** (-jnp.arange(ROPE_FREQS, dtype=F32) / ROPE_FREQS)
    ang_r = row[:, None] * inv_freq
    ang_c = col[:, None] * inv_freq
    zero = jnp.zeros_like(ang_r)
    cos = jnp.concatenate([jnp.cos(ang_r), jnp.cos(ang_r), jnp.cos(ang_c), jnp.cos(ang_c)], axis=1)
    s1 = jnp.concatenate([-jnp.sin(ang_r), zero, -jnp.sin(ang_c), zero], axis=1)
    s2 = jnp.concatenate([zero, jnp.sin(ang_r), zero, jnp.sin(ang_c)], axis=1)
    pad = lambda t, v: jnp.concatenate([jnp.full((n_ctx, HEAD_DIM), v, F32), t], axis=0)
    return pad(cos, 1.0), pad(s1, 0.0), pad(s2, 0.0)


def _route(aff, n_ctx, n_experts):
    nt = aff.shape[0]
    a_ctx = aff[:n_ctx, :n_experts].T
    a_lat = aff[n_ctx:, :n_experts].T
    g_l, i_l = lax.top_k(a_lat, CAPACITY_FACTOR * (nt - n_ctx) // n_experts)
    g_c, i_c = lax.top_k(a_ctx, CAPACITY_FACTOR * n_ctx // n_experts)
    idx = jnp.concatenate([i_l + n_ctx, i_c], axis=1).astype(jnp.int32)
    gates = jnp.concatenate([g_l, g_c], axis=1)
    return idx, gates


def kernel(x, c, ctx, c_ctx, mod_down, mod_up, mod_bias, norm_mix, norm_ffn, norm_final, a_w_qkv, a_w_o, a_sink, b_w_qkv, b_w_o, b_q_norm, b_k_norm, c_w_in, c_conv_w, c_conv_b, c_w_ra, c_b_ra, c_w_ix, c_b_ix, c_lambda, c_w_out, moe_router, moe_w_gate, moe_w_up, moe_w_down):
    assert x.shape[0] == 1 and ctx.shape[0] == 1
    seq, d = x.shape[1], x.shape[2]
    n_ctx = ctx.shape[1]
    depth = mod_down.shape[0]
    n_experts = moe_router.shape[2]
    n_heads = a_w_o.shape[1] // HEAD_DIM
    n_kv = n_heads // GROUP
    d_rnn = c_w_out.shape[1]

    xs = jnp.concatenate([ctx[0], x[0]], axis=0)
    cond = jnp.concatenate([c, c_ctx[None, :], jnp.zeros((6, d), F32)], axis=0)
    rope = _rope_tables(seq, n_ctx)
    zero_bias = jnp.zeros((1, mod_down.shape[2]), F32)

    for i in range(depth):
        low = _small_mm(cond, mod_down[i], zero_bias, silu=True)
        mod = _small_mm(low, mod_up[i], mod_bias[i][None, :], silu=False)[:2].reshape(2, 6, d)
        sh1, sc1, g1, sh2, sc2, g2 = (mod[:, k, :] for k in range(6))
        gain_mix = norm_mix[i][None, :]
        kind, j = i % 3, i // 3
        if kind == 0:
            qkv = _norm_mm(xs, gain_mix, sh1, sc1, a_w_qkv[j].astype(BF16), n_ctx=n_ctx, mode="rope",
                           out_dtype=BF16, rope=rope, n_rope_cols=(n_heads + n_kv) * HEAD_DIM)
            y = _attn_win(qkv, a_sink[j], n_ctx=n_ctx, n_heads=n_heads, n_kv=n_kv)
            w_out = a_w_o[j]
        elif kind == 1:
            head_gain = jnp.concatenate([jnp.tile(b_q_norm[j], n_heads), jnp.tile(b_k_norm[j], n_kv),
                                         jnp.ones((n_kv * HEAD_DIM,), F32)])[None, :]
            qkv = _norm_mm(xs, gain_mix, sh1, sc1, b_w_qkv[j].astype(BF16), n_ctx=n_ctx, mode="qknorm_rope",
                           out_dtype=BF16, rope=rope, head_gain=head_gain,
                           n_rope_cols=(n_heads + n_kv) * HEAD_DIM)
            y = _attn_full(qkv, n_ctx=n_ctx, n_heads=n_heads, n_kv=n_kv)
            w_out = b_w_o[j]
        else:
            gate = _norm_mm(xs, gain_mix, sh1, sc1, c_w_in[j][:, :d_rnn].astype(BF16), n_ctx=n_ctx,
                            mode="gelu", out_dtype=BF16)
            u = _norm_mm(xs, gain_mix, sh1, sc1, c_w_in[j][:, d_rnn:].astype(BF16), n_ctx=n_ctx,
                         mode="none", out_dtype=F32)
            y = _rglru(u, gate, c_conv_w[j], c_conv_b[j][None, :], c_w_ra[j].astype(BF16),
                       c_w_ix[j].astype(BF16), c_b_ra[j], c_b_ix[j], c_lambda[j], n_ctx=n_ctx)
            w_out = c_w_out[j]
        xs = _resid_mm(y, w_out.astype(BF16), xs, g1, n_ctx=n_ctx)

        w_router = jnp.pad(moe_router[i], ((0, 0), (0, LANES - n_experts))).astype(BF16)
        hp, aff = _ffn_pre(xs, norm_ffn[i][None, :], sh2, sc2, w_router, n_ctx=n_ctx, n_experts=n_experts)
        idx, gates = _route(aff, n_ctx, n_experts)
        yexp = _moe(idx, hp, gates[:, :, None], moe_w_gate[i], moe_w_up[i], moe_w_down[i])
        ffn = jnp.zeros_like(xs).at[idx.reshape(-1)].add(yexp.reshape(-1, d))
        is_ctx = (jnp.arange(xs.shape[0]) < n_ctx)[:, None]
        xs = xs + jnp.where(is_ctx, g2[1][None, :], g2[0][None, :]) * ffn

    return _final_norm(xs, norm_final[None, :], n_ctx=n_ctx)[None]
```

```python
import functools
import math

import jax
import jax.numpy as jnp
from jax import lax
from jax.experimental import pallas as pl
from jax.experimental.pallas import tpu as pltpu

F32 = jnp.float32
BF16 = jnp.bfloat16

HEAD_DIM = 128
GROUP = 4
WINDOW = 128
GRID_W = 64
ROPE_THETA = 10000.0
ROPE_FREQS = HEAD_DIM // 4
CAPACITY_FACTOR = 2
LRU_C = 8.0
EPS = 1e-6
NEG = -1e30

V7X_VMEM_BYTES = 64 * 1024 * 1024
VMEM_LIMIT = V7X_VMEM_BYTES - 8 * 1024 * 1024
LANES = 128
SUBLANES = 8


def _pick(n, candidates):
    for c in candidates:
        if c <= n and n % c == 0:
            return c
    return n


def _params(sem):
    return pltpu.CompilerParams(dimension_semantics=sem, vmem_limit_bytes=VMEM_LIMIT)


def _small_mm_kernel(a_ref, w_ref, b_ref, o_ref, *, silu):
    a = a_ref[...]
    if silu:
        a = a * jax.nn.sigmoid(a)
    acc = jnp.dot(a.astype(BF16), w_ref[...].astype(BF16), preferred_element_type=F32)
    o_ref[...] = acc + b_ref[...]


def _small_mm(a, w, bias, *, silu):
    m, k = a.shape
    n = w.shape[1]
    tn = _pick(n, (2048, 1024, 512, 256, 128))
    return pl.pallas_call(
        functools.partial(_small_mm_kernel, silu=silu),
        out_shape=jax.ShapeDtypeStruct((m, n), F32),
        grid=(n // tn,),
        in_specs=[pl.BlockSpec((m, k), lambda j: (0, 0)),
                  pl.BlockSpec((k, tn), lambda j: (0, j)),
                  pl.BlockSpec((1, tn), lambda j: (0, j))],
        out_specs=pl.BlockSpec((m, tn), lambda j: (0, j)),
        compiler_params=_params(("arbitrary",)),
        name="small_mm",
    )(a, w, bias)


ROW_CHUNK = 64


def _modulated_rows(x_ref, gain_ref, sh_ref, sc_ref, row0, n_ctx, emit):
    tm = x_ref.shape[0]

    def body(r, carry):
        r0 = pl.multiple_of(r * ROW_CHUNK, ROW_CHUNK)
        x = x_ref[pl.ds(r0, ROW_CHUNK), :]
        ms = jnp.mean(x * x, axis=-1, keepdims=True)
        y = (x * lax.rsqrt(ms + EPS)) * gain_ref[...]
        rows = row0 + r0 + lax.broadcasted_iota(jnp.int32, (ROW_CHUNK, 1), 0)
        is_ctx = rows < n_ctx
        sc = jnp.where(is_ctx, sc_ref[1:2, :], sc_ref[0:1, :])
        sh = jnp.where(is_ctx, sh_ref[1:2, :], sh_ref[0:1, :])
        emit(r0, y * (1.0 + sc) + sh)
        return carry

    lax.fori_loop(0, tm // ROW_CHUNK, body, 0)


def _rope(a, cos, s1, s2):
    return a * cos + pltpu.roll(a, HEAD_DIM - ROPE_FREQS, 1) * s1 + pltpu.roll(a, ROPE_FREQS, 1) * s2


def _norm_mm_kernel(*refs, n_ctx, tm, tn, mode, n_rope_tiles, n_q_tiles):
    if mode in ("rope", "qknorm_rope"):
        if mode == "qknorm_rope":
            x_ref, gain_ref, sh_ref, sc_ref, w_ref, cos_ref, s1_ref, s2_ref, hg_ref, o_ref, h_scr = refs
        else:
            x_ref, gain_ref, sh_ref, sc_ref, w_ref, cos_ref, s1_ref, s2_ref, o_ref, h_scr = refs
    else:
        x_ref, gain_ref, sh_ref, sc_ref, w_ref, o_ref, h_scr = refs
    i = pl.program_id(0)
    j = pl.program_id(1)

    @pl.when(j == 0)
    def _():
        def emit(r0, h):
            h_scr[pl.ds(r0, ROW_CHUNK), :] = h.astype(BF16)

        _modulated_rows(x_ref, gain_ref, sh_ref, sc_ref, i * tm, n_ctx, emit)

    acc = jnp.dot(h_scr[...], w_ref[...], preferred_element_type=F32)

    if mode == "none":
        o_ref[...] = acc.astype(o_ref.dtype)
    elif mode == "gelu":
        o_ref[...] = jax.nn.gelu(acc).astype(o_ref.dtype)
    else:
        @pl.when(j < n_rope_tiles)
        def _():
            for s in range(tn // HEAD_DIM):
                a = acc[:, s * HEAD_DIM:(s + 1) * HEAD_DIM]
                if mode == "qknorm_rope":
                    ms = jnp.mean(a * a, axis=-1, keepdims=True)
                    a = (a * lax.rsqrt(ms + EPS)) * hg_ref[:, s * HEAD_DIM:(s + 1) * HEAD_DIM]
                a = _rope(a, cos_ref[...], s1_ref[...], s2_ref[...])
                a = a * jnp.where(j < n_q_tiles, QK_FOLD, 1.0)
                o_ref[:, s * HEAD_DIM:(s + 1) * HEAD_DIM] = a.astype(o_ref.dtype)

        @pl.when(j >= n_rope_tiles)
        def _():
            o_ref[...] = acc.astype(o_ref.dtype)


def _norm_mm(xs, gain, sh, sc, w, *, n_ctx, mode, out_dtype, rope=None, head_gain=None, n_rope_cols=0,
             n_q_cols=0):
    nt, d = xs.shape
    n = w.shape[1]
    tm = _pick(nt, (768, 512, 256))
    tn = _pick(math.gcd(math.gcd(n, n_rope_cols), n_q_cols), (512, 256, 128))
    in_specs = [pl.BlockSpec((tm, d), lambda i, j: (i, 0)),
                pl.BlockSpec((1, d), lambda i, j: (0, 0)),
                pl.BlockSpec((2, d), lambda i, j: (0, 0)),
                pl.BlockSpec((2, d), lambda i, j: (0, 0)),
                pl.BlockSpec((d, tn), lambda i, j: (0, j))]
    args = [xs, gain, sh, sc, w]
    if mode in ("rope", "qknorm_rope"):
        assert n_rope_cols % tn == 0
        for t in rope:
            in_specs.append(pl.BlockSpec((tm, HEAD_DIM), lambda i, j: (i, 0)))
            args.append(t)
        if mode == "qknorm_rope":
            in_specs.append(pl.BlockSpec((1, tn), lambda i, j: (0, j)))
            args.append(head_gain)
    return pl.pallas_call(
        functools.partial(_norm_mm_kernel, n_ctx=n_ctx, tm=tm, tn=tn, mode=mode,
                          n_rope_tiles=n_rope_cols // tn, n_q_tiles=n_q_cols // tn),
        out_shape=jax.ShapeDtypeStruct((nt, n), out_dtype),
        grid=(nt // tm, n // tn),
        in_specs=in_specs,
        out_specs=pl.BlockSpec((tm, tn), lambda i, j: (i, j)),
        scratch_shapes=[pltpu.VMEM((tm, d), BF16)],
        compiler_params=_params(("arbitrary", "arbitrary")),
        name="norm_mm_" + mode,
    )(*args)


def _resid_mm_kernel(a_ref, w_ref, r_ref, g_ref, o_ref, *, n_ctx, tm):
    i = pl.program_id(0)
    acc = jnp.dot(a_ref[...], w_ref[...], preferred_element_type=F32)
    rows = i * tm + lax.broadcasted_iota(jnp.int32, (tm, 1), 0)
    g = jnp.where(rows < n_ctx, g_ref[1:2, :], g_ref[0:1, :])
    o_ref[...] = r_ref[...] + g * acc


def _resid_mm(a, w, resid, gate, *, n_ctx):
    nt, k = a.shape
    n = w.shape[1]
    tm = _pick(nt, (1056, 768, 640, 512, 256))
    tn = _pick(n, (512, 256, 128))
    return pl.pallas_call(
        functools.partial(_resid_mm_kernel, n_ctx=n_ctx, tm=tm),
        out_shape=jax.ShapeDtypeStruct((nt, n), F32),
        grid=(nt // tm, n // tn),
        in_specs=[pl.BlockSpec((tm, k), lambda i, j: (i, 0)),
                  pl.BlockSpec((k, tn), lambda i, j: (0, j)),
                  pl.BlockSpec((tm, tn), lambda i, j: (i, j)),
                  pl.BlockSpec((2, tn), lambda i, j: (0, j))],
        out_specs=pl.BlockSpec((tm, tn), lambda i, j: (i, j)),
        compiler_params=_params(("arbitrary", "arbitrary")),
        name="resid_mm",
    )(a, w, resid, gate)


QK_FOLD = HEAD_DIM ** -0.5 * math.log2(math.e)
_NT_DIMS = (((1,), (1,)), ((), ()))


def _stack_heads(q_ref, r0, tq):
    return jnp.concatenate([q_ref[r0:r0 + tq, g * HEAD_DIM:(g + 1) * HEAD_DIM] for g in range(GROUP)], axis=0)


def _with_ones(v):
    return jnp.concatenate([v, jnp.ones(v.shape, v.dtype)], axis=1)


def _attn_win_kernel(sink_ref, q_ref, k_ref, v_ref, o_ref, s_buf, *, tq, nq, n_ctx, wk):
    h = pl.program_id(0)
    step = pl.program_id(1)
    nt = k_ref.shape[0]
    n_ctx_tiles = n_ctx // tq

    def window(t):
        tile = step * nq + t
        start = pl.multiple_of(jnp.clip(tile * tq - WINDOW, 0, nt - wk), WINDOW)
        return tile, start

    def s_tile(t):
        _, start = window(t)
        k_all = jnp.concatenate([k_ref[0:n_ctx, :], k_ref[pl.ds(start, wk), :]], axis=0)
        s_buf[t % 2] = lax.dot_general(_stack_heads(q_ref, t * tq, tq), k_all, _NT_DIMS,
                                       preferred_element_type=F32)

    def consume(t):
        tile, start = window(t)
        v_aug = _with_ones(jnp.concatenate([v_ref[0:n_ctx, :], v_ref[pl.ds(start, wk), :]], axis=0))
        c = lax.broadcasted_iota(jnp.int32, (tq, wk), 1)
        r = lax.broadcasted_iota(jnp.int32, (tq, wk), 0)
        rel = (c - r) + ((start - n_ctx) - (tile - n_ctx_tiles) * tq)
        reach = jnp.where(tile >= n_ctx_tiles, WINDOW, -1)
        bias = jnp.where(jnp.abs(rel) <= reach, jnp.where(c >= n_ctx - start, 0.0, NEG), NEG)
        bias = jnp.concatenate([jnp.zeros((tq, n_ctx), F32), bias], axis=1)
        for g in range(GROUP):
            s = s_buf[t % 2, g * tq:(g + 1) * tq, :] + bias
            sk = sink_ref[h * GROUP + g] * math.log2(math.e)
            m = jnp.maximum(jnp.max(s, axis=1, keepdims=True), sk)
            p = jnp.exp2(s - m)
            acc = jnp.dot(p.astype(BF16), v_aug, preferred_element_type=F32)
            o = acc[:, :HEAD_DIM] / (acc[:, HEAD_DIM:] + jnp.exp2(sk - m))
            o_ref[t * tq:(t + 1) * tq, g * HEAD_DIM:(g + 1) * HEAD_DIM] = o.astype(o_ref.dtype)

    s_tile(0)
    for t in range(nq):
        if t + 1 < nq:
            s_tile(t + 1)
        consume(t)


def _attn_win(qkv, sink, *, n_ctx, n_heads, n_kv):
    nt = qkv.shape[0]
    tq = n_ctx
    wk = tq + 2 * WINDOW
    assert tq % WINDOW == 0 and nt % tq == 0 and nt >= wk
    nq = _pick(nt // tq, (3, 2, 1))
    kcol, vcol = n_heads, n_heads + n_kv
    return pl.pallas_call(
        functools.partial(_attn_win_kernel, tq=tq, nq=nq, n_ctx=n_ctx, wk=wk),
        out_shape=jax.ShapeDtypeStruct((nt, n_heads * HEAD_DIM), BF16),
        grid=(n_kv, nt // (nq * tq)),
        in_specs=[pl.BlockSpec(memory_space=pltpu.SMEM),
                  pl.BlockSpec((nq * tq, GROUP * HEAD_DIM), lambda h, i: (i, h)),
                  pl.BlockSpec((nt, HEAD_DIM), lambda h, i: (0, kcol + h)),
                  pl.BlockSpec((nt, HEAD_DIM), lambda h, i: (0, vcol + h))],
        out_specs=pl.BlockSpec((nq * tq, GROUP * HEAD_DIM), lambda h, i: (i, h)),
        scratch_shapes=[pltpu.VMEM((2, GROUP * tq, n_ctx + wk), F32)],
        compiler_params=_params(("arbitrary", "arbitrary")),
        name="attn_win",
    )(sink, qkv, qkv, qkv)


def _attn_full_kernel(q_ref, k_ref, v_ref, o_ref, s_buf, m_scr, acc_scr, *, tq, tk, n_ctx):
    i = pl.program_id(1)
    nt = k_ref.shape[0]
    n_chunks = nt // tk

    def s_tile(c, slot):
        off = pl.multiple_of(c * tk, tk)
        s_buf[slot] = lax.dot_general(_stack_heads(q_ref, 0, tq), k_ref[pl.ds(off, tk), :], _NT_DIMS,
                                      preferred_element_type=F32)

    def consume(c, slot):
        off = pl.multiple_of(c * tk, tk)
        v_aug = _with_ones(v_ref[pl.ds(off, tk), :])
        for g in range(GROUP):
            rows = slice(g * tq, (g + 1) * tq)
            s = s_buf[slot, rows, :]
            m_prev = m_scr[rows, :]
            m_new = jnp.maximum(m_prev, jnp.max(s, axis=1, keepdims=True))
            p = jnp.exp2(s - m_new)
            acc_scr[rows, :] = (jnp.exp2(m_prev - m_new) * acc_scr[rows, :]
                                + jnp.dot(p.astype(BF16), v_aug, preferred_element_type=F32))
            m_scr[rows, :] = m_new

    def finish():
        acc = acc_scr[...]
        o = acc[:, :HEAD_DIM] / acc[:, HEAD_DIM:]
        for g in range(GROUP):
            o_ref[:, g * HEAD_DIM:(g + 1) * HEAD_DIM] = o[g * tq:(g + 1) * tq].astype(o_ref.dtype)

    @pl.when(i < n_ctx // tq)
    def _():
        s = lax.dot_general(_stack_heads(q_ref, 0, tq), k_ref[0:n_ctx, :], _NT_DIMS, preferred_element_type=F32)
        p = jnp.exp2(s - jnp.max(s, axis=1, keepdims=True))
        acc_scr[...] = jnp.dot(p.astype(BF16), _with_ones(v_ref[0:n_ctx, :]), preferred_element_type=F32)
        finish()

    @pl.when(i >= n_ctx // tq)
    def _():
        m_scr[...] = jnp.full(m_scr.shape, NEG, F32)
        acc_scr[...] = jnp.zeros(acc_scr.shape, F32)
        s_tile(0, 0)

        def pair(j, carry):
            s_tile(2 * j + 1, 1)
            consume(2 * j, 0)
            s_tile(2 * j + 2, 0)
            consume(2 * j + 1, 1)
            return carry

        lax.fori_loop(0, (n_chunks - 1) // 2, pair, 0)
        if n_chunks % 2 == 1:
            consume(n_chunks - 1, 0)
        else:
            s_tile(n_chunks - 1, 1)
            consume(n_chunks - 2, 0)
            consume(n_chunks - 1, 1)
        finish()


def _attn_full(qkv, *, n_ctx, n_heads, n_kv):
    nt = qkv.shape[0]
    tq = _pick(n_ctx, (256, 128))
    tk = _pick(nt, (768, 512, 384, 256, 128))
    assert nt % tq == 0 and n_ctx % tq == 0
    kcol, vcol = n_heads, n_heads + n_kv
    return pl.pallas_call(
        functools.partial(_attn_full_kernel, tq=tq, tk=tk, n_ctx=n_ctx),
        out_shape=jax.ShapeDtypeStruct((nt, n_heads * HEAD_DIM), BF16),
        grid=(n_kv, nt // tq),
        in_specs=[pl.BlockSpec((tq, GROUP * HEAD_DIM), lambda h, i: (i, h)),
                  pl.BlockSpec((nt, HEAD_DIM), lambda h, i: (0, kcol + h)),
                  pl.BlockSpec((nt, HEAD_DIM), lambda h, i: (0, vcol + h))],
        out_specs=pl.BlockSpec((tq, GROUP * HEAD_DIM), lambda h, i: (i, h)),
        scratch_shapes=[pltpu.VMEM((2, GROUP * tq, tk), F32), pltpu.VMEM((GROUP * tq, 1), F32),
                        pltpu.VMEM((GROUP * tq, 2 * HEAD_DIM), F32)],
        compiler_params=_params(("arbitrary", "arbitrary")),
        name="attn_full",
    )(qkv, qkv, qkv)


def _rglru_kernel(u_ref, gate_ref, cw_ref, cb_ref, wr_ref, wi_ref, br_ref, bi_ref, lam_ref, o_ref, rec_scr,
                  *, tc, n_chunks, n_ctx_chunks):
    nt, ch = u_ref.shape
    row = lax.broadcasted_iota(jnp.int32, (tc, 1), 0)

    def conv_chunk(c):
        t0 = pl.multiple_of(c * tc, tc)
        x0 = u_ref[pl.ds(t0, tc), :]
        seg_start = (c == 0) | (c == n_ctx_chunks)
        seg_end = (c == n_ctx_chunks - 1) | (c == n_chunks - 1)
        p0 = pl.multiple_of(jnp.maximum(t0 - 8, 0), 8)
        n0 = pl.multiple_of(jnp.minimum(t0 + tc, nt - 8), 8)
        prev = u_ref[pl.ds(p0, 8), :] * jnp.where(seg_start, 0.0, 1.0)
        nxt = u_ref[pl.ds(n0, 8), :] * jnp.where(seg_end, 0.0, 1.0)
        xm1 = jnp.where(row == 0, prev[7:8, :], pltpu.roll(x0, 1, 0))
        xm2 = jnp.where(row == 0, prev[6:7, :], jnp.where(row == 1, prev[7:8, :], pltpu.roll(x0, 2, 0)))
        xp1 = jnp.where(row == tc - 1, nxt[0:1, :], pltpu.roll(x0, tc - 1, 0))
        uc = cw_ref[0:1, :] * xm2 + cw_ref[1:2, :] * xm1 + cw_ref[2:3, :] * x0 + cw_ref[3:4, :] * xp1
        return t0, uc + cb_ref[...]

    def coeffs(uc, d):
        ub = uc.astype(BF16)
        r = jax.nn.sigmoid(jnp.dot(ub, wr_ref[d, 0], preferred_element_type=F32) + br_ref[d:d + 1, :])
        gi = jax.nn.sigmoid(jnp.dot(ub, wi_ref[d, 0], preferred_element_type=F32) + bi_ref[d:d + 1, :])
        x = -lam_ref[d:d + 1, :]
        softplus = jnp.maximum(x, 0.0) + jnp.log1p(jnp.exp(-jnp.abs(x)))
        log_a = (-LRU_C * r) * softplus
        a = jnp.exp(log_a)
        one_minus_a2 = -jnp.tanh(log_a) * (a * a + 1.0)
        return a, jnp.sqrt(one_minus_a2) * (gi * uc)

    def scan_chunk(a, b, h0, reverse):
        first = tc - 1 if reverse else 0
        b = jnp.where(row == first, b + a * h0, b)
        s = 1
        while s < tc:
            if reverse:
                keep = row < tc - s
                a_sh, b_sh = pltpu.roll(a, tc - s, 0), pltpu.roll(b, tc - s, 0)
            else:
                keep = row >= s
                a_sh, b_sh = pltpu.roll(a, s, 0), pltpu.roll(b, s, 0)
            b = a * jnp.where(keep, b_sh, 0.0) + b
            a = a * jnp.where(keep, a_sh, 1.0)
            s *= 2
        return b

    def fwd(c, h):
        t0, uc = conv_chunk(c)
        a, b = coeffs(uc, 0)
        hh = scan_chunk(a, b, h, False)
        rec_scr[pl.ds(t0, tc), :] = hh
        return hh[tc - 1:tc, :]

    def bwd(j, h):
        c = jnp.where(j < n_ctx_chunks, n_ctx_chunks - 1 - j, n_chunks - 1 - (j - n_ctx_chunks))
        t0, uc = conv_chunk(c)
        a, b = coeffs(uc, 1)
        hh = scan_chunk(a, b, h, True)
        rec = rec_scr[pl.ds(t0, tc), :] + hh
        o_ref[pl.ds(t0, tc), :] = (gate_ref[pl.ds(t0, tc), :].astype(F32) * rec).astype(o_ref.dtype)
        return hh[0:1, :]

    zero = jnp.zeros((1, ch), F32)
    lax.fori_loop(0, n_chunks, fwd, zero)
    lax.fori_loop(0, n_chunks, bwd, zero)


def _rglru(u, gate, conv_w, conv_b, w_ra, w_ix, b_ra, b_ix, lam, *, n_ctx):
    nt, d = u.shape
    nb, ch = w_ra.shape[1], w_ra.shape[2]
    tc = _pick(n_ctx, (256, 128))
    assert n_ctx % tc == 0 and nt % tc == 0
    col = lambda h: (0, h)
    return pl.pallas_call(
        functools.partial(_rglru_kernel, tc=tc, n_chunks=nt // tc, n_ctx_chunks=n_ctx // tc),
        out_shape=jax.ShapeDtypeStruct((nt, d), BF16),
        grid=(nb,),
        in_specs=[pl.BlockSpec((nt, ch), col),
                  pl.BlockSpec((nt, ch), col),
                  pl.BlockSpec((conv_w.shape[0], ch), col),
                  pl.BlockSpec((1, ch), col),
                  pl.BlockSpec((2, 1, ch, ch), lambda h: (0, h, 0, 0)),
                  pl.BlockSpec((2, 1, ch, ch), lambda h: (0, h, 0, 0)),
                  pl.BlockSpec((2, ch), col),
                  pl.BlockSpec((2, ch), col),
                  pl.BlockSpec((2, ch), col)],
        out_specs=pl.BlockSpec((nt, ch), col),
        scratch_shapes=[pltpu.VMEM((nt, ch), F32)],
        compiler_params=_params(("arbitrary",)),
        name="rglru",
    )(u, gate, conv_w, conv_b, w_ra, w_ix, b_ra, b_ix, lam)


def _ffn_pre_kernel(x_ref, gain_ref, sh_ref, sc_ref, wr_ref, hp_ref, aff_ref, *, n_ctx, tm, n_experts):
    i = pl.program_id(0)
    half = x_ref.shape[1] // 2

    def emit(r0, h):
        hb = h.astype(BF16)
        bits = pltpu.bitcast(hb.astype(F32), jnp.uint32)
        hp_ref[pl.ds(r0, ROW_CHUNK), :] = (bits[:, :half] & jnp.uint32(0xFFFF0000)) | (bits[:, half:] >> 16)
        logits = jnp.dot(hb, wr_ref[...], preferred_element_type=F32)
        lane = lax.broadcasted_iota(jnp.int32, logits.shape, 1)
        logits = jnp.where(lane < n_experts, logits, NEG)
        e = jnp.exp(logits - jnp.max(logits, axis=-1, keepdims=True))
        aff_ref[pl.ds(r0, ROW_CHUNK), :] = e / jnp.sum(e, axis=-1, keepdims=True)

    _modulated_rows(x_ref, gain_ref, sh_ref, sc_ref, i * tm, n_ctx, emit)


def _ffn_pre(xs, gain, sh, sc, w_router_pad, *, n_ctx, n_experts):
    nt, d = xs.shape
    tm = _pick(nt, (768, 512, 256))
    return pl.pallas_call(
        functools.partial(_ffn_pre_kernel, n_ctx=n_ctx, tm=tm, n_experts=n_experts),
        out_shape=(jax.ShapeDtypeStruct((nt, d // 2), jnp.uint32),
                   jax.ShapeDtypeStruct((nt, LANES), F32)),
        grid=(nt // tm,),
        in_specs=[pl.BlockSpec((tm, d), lambda i: (i, 0)),
                  pl.BlockSpec((1, d), lambda i: (0, 0)),
                  pl.BlockSpec((2, d), lambda i: (0, 0)),
                  pl.BlockSpec((2, d), lambda i: (0, 0)),
                  pl.BlockSpec((d, LANES), lambda i: (0, 0))],
        out_specs=(pl.BlockSpec((tm, d // 2), lambda i: (i, 0)),
                   pl.BlockSpec((tm, LANES), lambda i: (i, 0))),
        compiler_params=_params(("arbitrary",)),
        name="ffn_pre",
    )(xs, gain, sh, sc, w_router_pad)


def _pack_bf16_pair(hi, lo):
    hi_bits = pltpu.bitcast(hi.astype(BF16).astype(F32), jnp.uint32)
    lo_bits = pltpu.bitcast(lo.astype(BF16).astype(F32), jnp.uint32)
    return (hi_bits & jnp.uint32(0xFFFF0000)) | (lo_bits >> 16)


def _unpack_bf16_pair(w):
    return (pltpu.bitcast(w & jnp.uint32(0xFFFF0000), F32).astype(BF16),
            pltpu.bitcast(w << 16, F32).astype(BF16))


def _moe_kernel(idx_ref, hp_hbm, g_ref, wg_ref, wu_ref, wda_ref, wdb_ref, o_ref, land, xb, act, sem, *, nf, tf):
    e = pl.program_id(0)
    s = pl.program_id(1)
    n_experts = pl.num_programs(0)
    rows, half = land.shape

    def gather_rows(expert):
        def body(p, carry):
            src = idx_ref[expert, p]
            pltpu.make_async_copy(hp_hbm.at[pl.ds(src, 1), :], land.at[pl.ds(p, 1), :], sem).start()
            return carry

        lax.fori_loop(0, rows, body, 0)

    @pl.when((e == 0) & (s == 0))
    def _():
        gather_rows(0)

    @pl.when(s == 0)
    def _():
        pltpu.make_async_copy(hp_hbm.at[pl.ds(0, rows), :], land, sem).wait()
        xb[:, :half], xb[:, half:] = _unpack_bf16_pair(land[...])

        @pl.when(e + 1 < n_experts)
        def _():
            gather_rows(e + 1)

    @pl.when(s < nf)
    def _():
        x = xb[...]
        a = jnp.dot(x, wg_ref[0, 0].astype(BF16), preferred_element_type=F32)
        u = jnp.dot(x, wu_ref[0, 0].astype(BF16), preferred_element_type=F32)
        act[s] = (a * jax.nn.sigmoid(a) * u).astype(BF16)

    @pl.when(s >= nf)
    def _():
        def down(wd_ref):
            acc = jnp.dot(act[0], wd_ref[0, 0, 0:tf, :].astype(BF16), preferred_element_type=F32)
            for f in range(1, nf):
                acc = acc + jnp.dot(act[f], wd_ref[0, 0, f * tf:(f + 1) * tf, :].astype(BF16),
                                    preferred_element_type=F32)
            return acc * g_ref[0]

        o_ref[0] = _pack_bf16_pair(down(wda_ref), down(wdb_ref))


def _moe(idx, hp, gates, w_gate, w_up, w_down, layer):
    n_experts, rows = idx.shape
    half = hp.shape[1]
    dexp = w_gate.shape[3]
    tf = _pick(dexp, (256, 128))
    tp = _pick(half, (256, 128))
    nf, nd = dexp // tf, half // tp
    up_map = lambda e, s, idx: (layer, e, 0, jnp.minimum(s, nf - 1))
    grid_spec = pltpu.PrefetchScalarGridSpec(
        num_scalar_prefetch=1,
        grid=(n_experts, nf + nd),
        in_specs=[pl.BlockSpec(memory_space=pl.ANY),
                  pl.BlockSpec((1, rows, 1), lambda e, s, idx: (e, 0, 0)),
                  pl.BlockSpec((1, 1, 2 * half, tf), up_map),
                  pl.BlockSpec((1, 1, 2 * half, tf), up_map),
                  pl.BlockSpec((1, 1, dexp, tp), lambda e, s, idx: (layer, e, 0, jnp.maximum(s - nf, 0))),
                  pl.BlockSpec((1, 1, dexp, tp), lambda e, s, idx: (layer, e, 0, jnp.maximum(s - nf, 0) + nd))],
        out_specs=pl.BlockSpec((1, rows, tp), lambda e, s, idx: (e, 0, jnp.maximum(s - nf, 0))),
        scratch_shapes=[pltpu.VMEM((rows, half), jnp.uint32),
                        pltpu.VMEM((rows, 2 * half), BF16),
                        pltpu.VMEM((nf, rows, tf), BF16),
                        pltpu.SemaphoreType.DMA(())],
    )
    return pl.pallas_call(
        functools.partial(_moe_kernel, nf=nf, tf=tf),
        out_shape=jax.ShapeDtypeStruct((n_experts, rows, half), jnp.uint32),
        grid_spec=grid_spec,
        compiler_params=_params(("arbitrary", "arbitrary")),
        name="moe_experts",
    )(idx, hp, gates, w_gate, w_up, w_down, w_down)


COMBINE_TB = 256
COMBINE_W = 64


def _combine_kernel(src_ref, nwin_ref, y_hbm, wt_ref, x_ref, g_ref, o_ref, ybuf, yextra, sem, sem_extra,
                    *, n_experts, rows, n_ctx):
    b = pl.program_id(0)
    n_blocks = pl.num_programs(0)
    tb, d = x_ref.shape
    half = d // 2
    n_windows = wt_ref.shape[1]
    w = COMBINE_W
    slot = lax.rem(b, 2)

    def fetch(block, window, dst, dma_sem):
        for e in range(n_experts):
            src = pl.multiple_of(e * rows + src_ref[(block * n_windows + window) * n_experts + e], SUBLANES)
            pltpu.make_async_copy(y_hbm.at[pl.ds(src, w), :], dst.at[pl.ds(e * w, w), :], dma_sem).start()

    def wait(dst, dma_sem):
        pltpu.make_async_copy(y_hbm.at[pl.ds(0, n_experts * w), :], dst, dma_sem).wait()

    def expand(buf, window, first):
        wt = wt_ref[0, pl.ds(window, 1), :]
        tok = lax.broadcasted_iota(jnp.int32, (tb, n_experts * w), 0)
        onehot = jnp.where(wt == tok, 1.0, 0.0).astype(BF16)
        chunk = _pick(half, (512, 256, 128))
        for k in range(half // chunk):
            cols = slice(k * chunk, (k + 1) * chunk)
            cols_hi = slice(half + k * chunk, half + (k + 1) * chunk)
            hi, lo = _unpack_bf16_pair(buf[:, cols])
            a_hi = jnp.dot(onehot, hi, preferred_element_type=F32)
            a_lo = jnp.dot(onehot, lo, preferred_element_type=F32)
            if first:
                o_ref[:, cols] = a_hi
                o_ref[:, cols_hi] = a_lo
            else:
                o_ref[:, cols] += a_hi
                o_ref[:, cols_hi] += a_lo

    @pl.when(b == 0)
    def _():
        fetch(0, 0, ybuf.at[0], sem.at[0])

    wait(ybuf.at[slot], sem.at[slot])

    @pl.when(b + 1 < n_blocks)
    def _():
        fetch(b + 1, 0, ybuf.at[1 - slot], sem.at[1 - slot])

    expand(ybuf.at[slot], 0, True)

    def extra(window, carry):
        fetch(b, window, yextra, sem_extra)
        wait(yextra, sem_extra)
        expand(yextra, window, False)
        return carry

    lax.fori_loop(1, nwin_ref[b], extra, 0)

    row = b * tb + lax.broadcasted_iota(jnp.int32, (tb, 1), 0)
    g = jnp.where(row < n_ctx, g_ref[1:2, :], g_ref[0:1, :])
    o_ref[...] = x_ref[...] + g * o_ref[...]


def _combine_plan(rows_sorted, nt):
    n_experts, r = rows_sorted.shape
    tb, w = COMBINE_TB, COMBINE_W
    served = w - SUBLANES
    nb, nw = nt // tb, -(-tb // served)
    edges = jnp.arange(nb + 1, dtype=jnp.int32) * tb
    below = jnp.sum((rows_sorted[None, :, :] < edges[:, None, None]).astype(jnp.int32), axis=2)
    start, stop = below[:-1], below[1:]
    first = start[:, None, :] + jnp.arange(nw, dtype=jnp.int32)[None, :, None] * served
    src = jnp.minimum(first // SUBLANES * SUBLANES, r - w)
    slot = src[..., None] + jnp.arange(w, dtype=jnp.int32)
    last = jnp.minimum(first + served, stop[:, None, :])
    valid = (slot >= first[..., None]) & (slot < last[..., None])
    tok = rows_sorted[jnp.arange(n_experts)[None, None, :, None], slot]
    local = tok - edges[:-1][:, None, None, None]
    win_tok = jnp.where(valid, local, -1).reshape(nb, nw, n_experts * w)
    n_win = jnp.maximum(1, jnp.max((stop - start + served - 1) // served, axis=1)).astype(jnp.int32)
    return src.reshape(-1), n_win, win_tok


def _combine(y_packed, rows_sorted, xs, gate, *, n_ctx):
    nt, d = xs.shape
    n_experts, r, half = y_packed.shape
    tb = COMBINE_TB
    assert nt % tb == 0 and r >= COMBINE_W and r % SUBLANES == 0 and half % LANES == 0
    src, n_win, win_tok = _combine_plan(rows_sorted, nt)
    nw = win_tok.shape[1]
    grid_spec = pltpu.PrefetchScalarGridSpec(
        num_scalar_prefetch=2,
        grid=(nt // tb,),
        in_specs=[pl.BlockSpec(memory_space=pl.ANY),
                  pl.BlockSpec((1, nw, n_experts * COMBINE_W), lambda b, src, nwin: (b, 0, 0)),
                  pl.BlockSpec((tb, d), lambda b, src, nwin: (b, 0)),
                  pl.BlockSpec((2, d), lambda b, src, nwin: (0, 0))],
        out_specs=pl.BlockSpec((tb, d), lambda b, src, nwin: (b, 0)),
        scratch_shapes=[pltpu.VMEM((2, n_experts * COMBINE_W, half), jnp.uint32),
                        pltpu.VMEM((n_experts * COMBINE_W, half), jnp.uint32),
                        pltpu.SemaphoreType.DMA((2,)),
                        pltpu.SemaphoreType.DMA(())],
    )
    return pl.pallas_call(
        functools.partial(_combine_kernel, n_experts=n_experts, rows=r, n_ctx=n_ctx),
        out_shape=jax.ShapeDtypeStruct((nt, d), F32),
        grid_spec=grid_spec,
        compiler_params=_params(("arbitrary",)),
        name="moe_combine",
    )(src, n_win, y_packed.reshape(n_experts * r, half), win_tok, xs, gate)


def _final_norm_kernel(x_ref, gain_ref, o_ref):
    x = x_ref[...]
    ms = jnp.mean(x * x, axis=-1, keepdims=True)
    o_ref[...] = (x * lax.rsqrt(ms + EPS)) * gain_ref[...]


def _final_norm(xs, gain, *, n_ctx):
    nt, d = xs.shape
    tm = _pick(n_ctx, (256, 128))
    off = n_ctx // tm
    return pl.pallas_call(
        _final_norm_kernel,
        out_shape=jax.ShapeDtypeStruct((nt - n_ctx, d), F32),
        grid=((nt - n_ctx) // tm,),
        in_specs=[pl.BlockSpec((tm, d), lambda i: (i + off, 0)),
                  pl.BlockSpec((1, d), lambda i: (0, 0))],
        out_specs=pl.BlockSpec((tm, d), lambda i: (i, 0)),
        compiler_params=_params(("arbitrary",)),
        name="final_norm",
    )(xs, gain)


def _rope_tables(seq, n_ctx):
    rows = seq // GRID_W
    row = jnp.repeat(jnp.arange(rows), GRID_W).astype(F32)
    col = jnp.tile(jnp.arange(GRID_W), rows).astype(F32)
    inv_freq = ROPE_THETA ** (-jnp.arange(ROPE_FREQS, dtype=F32) / ROPE_FREQS)
    ang_r = row[:, None] * inv_freq
    ang_c = col[:, None] * inv_freq
    zero = jnp.zeros_like(ang_r)
    cos = jnp.concatenate([jnp.cos(ang_r), jnp.cos(ang_r), jnp.cos(ang_c), jnp.cos(ang_c)], axis=1)
    s1 = jnp.concatenate([-jnp.sin(ang_r), zero, -jnp.sin(ang_c), zero], axis=1)
    s2 = jnp.concatenate([zero, jnp.sin(ang_r), zero, jnp.sin(ang_c)], axis=1)
    pad = lambda t, v: jnp.concatenate([jnp.full((n_ctx, HEAD_DIM), v, F32), t], axis=0)
    return pad(cos, 1.0), pad(s1, 0.0), pad(s2, 0.0)


def _route(aff, n_ctx, n_experts):
    nt = aff.shape[0]
    a_ctx = aff[:n_ctx, :n_experts].T
    a_lat = aff[n_ctx:, :n_experts].T
    g_l, i_l = lax.top_k(a_lat, CAPACITY_FACTOR * (nt - n_ctx) // n_experts)
    g_c, i_c = lax.top_k(a_ctx, CAPACITY_FACTOR * n_ctx // n_experts)
    idx = jnp.concatenate([i_l + n_ctx, i_c], axis=1).astype(jnp.int32)
    gates = jnp.concatenate([g_l, g_c], axis=1)
    return lax.sort((idx, gates), dimension=1, num_keys=1)


def kernel(x, c, ctx, c_ctx, mod_down, mod_up, mod_bias, norm_mix, norm_ffn, norm_final, a_w_qkv, a_w_o, a_sink, b_w_qkv, b_w_o, b_q_norm, b_k_norm, c_w_in, c_conv_w, c_conv_b, c_w_ra, c_b_ra, c_w_ix, c_b_ix, c_lambda, c_w_out, moe_router, moe_w_gate, moe_w_up, moe_w_down):
    assert x.shape[0] == 1 and ctx.shape[0] == 1
    seq, d = x.shape[1], x.shape[2]
    n_ctx = ctx.shape[1]
    depth = mod_down.shape[0]
    n_experts = moe_router.shape[2]
    n_heads = a_w_o.shape[1] // HEAD_DIM
    n_kv = n_heads // GROUP
    d_rnn = c_w_out.shape[1]

    xs = jnp.concatenate([ctx[0], x[0]], axis=0)
    cond = jnp.concatenate([c, c_ctx[None, :], jnp.zeros((6, d), F32)], axis=0)
    rope = _rope_tables(seq, n_ctx)
    zero_bias = jnp.zeros((1, mod_down.shape[2]), F32)

    for i in range(depth):
        low = _small_mm(cond, mod_down[i], zero_bias, silu=True)
        mod = _small_mm(low, mod_up[i], mod_bias[i][None, :], silu=False)[:2].reshape(2, 6, d)
        sh1, sc1, g1, sh2, sc2, g2 = (mod[:, k, :] for k in range(6))
        gain_mix = norm_mix[i][None, :]
        kind, j = i % 3, i // 3
        if kind == 0:
            qkv = _norm_mm(xs, gain_mix, sh1, sc1, a_w_qkv[j].astype(BF16), n_ctx=n_ctx, mode="rope",
                           out_dtype=BF16, rope=rope, n_rope_cols=(n_heads + n_kv) * HEAD_DIM,
                           n_q_cols=n_heads * HEAD_DIM)
            y = _attn_win(qkv, a_sink[j], n_ctx=n_ctx, n_heads=n_heads, n_kv=n_kv)
            w_out = a_w_o[j]
        elif kind == 1:
            head_gain = jnp.concatenate([jnp.tile(b_q_norm[j], n_heads), jnp.tile(b_k_norm[j], n_kv),
                                         jnp.ones((n_kv * HEAD_DIM,), F32)])[None, :]
            qkv = _norm_mm(xs, gain_mix, sh1, sc1, b_w_qkv[j].astype(BF16), n_ctx=n_ctx, mode="qknorm_rope",
                           out_dtype=BF16, rope=rope, head_gain=head_gain,
                           n_rope_cols=(n_heads + n_kv) * HEAD_DIM, n_q_cols=n_heads * HEAD_DIM)
            y = _attn_full(qkv, n_ctx=n_ctx, n_heads=n_heads, n_kv=n_kv)
            w_out = b_w_o[j]
        else:
            gate = _norm_mm(xs, gain_mix, sh1, sc1, c_w_in[j][:, :d_rnn].astype(BF16), n_ctx=n_ctx,
                            mode="gelu", out_dtype=BF16)
            u = _norm_mm(xs, gain_mix, sh1, sc1, c_w_in[j][:, d_rnn:].astype(BF16), n_ctx=n_ctx,
                         mode="none", out_dtype=F32)
            y = _rglru(u, gate, c_conv_w[j], c_conv_b[j][None, :], c_w_ra[j].astype(BF16),
                       c_w_ix[j].astype(BF16), c_b_ra[j], c_b_ix[j], c_lambda[j], n_ctx=n_ctx)
            w_out = c_w_out[j]
        xs = _resid_mm(y, w_out.astype(BF16), xs, g1, n_ctx=n_ctx)

        w_router = jnp.pad(moe_router[i], ((0, 0), (0, LANES - n_experts))).astype(BF16)
        hp, aff = _ffn_pre(xs, norm_ffn[i][None, :], sh2, sc2, w_router, n_ctx=n_ctx, n_experts=n_experts)
        idx, gates = _route(aff, n_ctx, n_experts)
        yexp = _moe(idx, hp, gates[:, :, None], moe_w_gate, moe_w_up, moe_w_down, i)
        xs = _combine(yexp, idx, xs, g2, n_ctx=n_ctx)

    return _final_norm(xs, norm_final[None, :], n_ctx=n_ctx)[None]
```

```python
import functools
import math

import jax
import jax.numpy as jnp
from jax import lax
from jax.experimental import pallas as pl
from jax.experimental.pallas import tpu as pltpu

F32 = jnp.float32
BF16 = jnp.bfloat16

HEAD_DIM = 128
GROUP = 4
WINDOW = 128
GRID_W = 64
ROPE_THETA = 10000.0
ROPE_FREQS = HEAD_DIM // 4
CAPACITY_FACTOR = 2
LRU_C = 8.0
EPS = 1e-6
NEG = -1e30

V7X_VMEM_BYTES = 64 * 1024 * 1024
VMEM_LIMIT = V7X_VMEM_BYTES - 8 * 1024 * 1024
LANES = 128
SUBLANES = 8


def _pick(n, candidates):
    for c in candidates:
        if c <= n and n % c == 0:
            return c
    return n


def _params(sem):
    return pltpu.CompilerParams(dimension_semantics=sem, vmem_limit_bytes=VMEM_LIMIT)


def _small_mm_kernel(a_ref, w_ref, b_ref, o_ref, *, silu):
    a = a_ref[...]
    if silu:
        a = a * jax.nn.sigmoid(a)
    acc = jnp.dot(a.astype(BF16), w_ref[...].astype(BF16), preferred_element_type=F32)
    o_ref[...] = acc + b_ref[...]


def _small_mm(a, w, bias, *, silu):
    m, k = a.shape
    n = w.shape[1]
    tn = _pick(n, (2048, 1024, 512, 256, 128))
    return pl.pallas_call(
        functools.partial(_small_mm_kernel, silu=silu),
        out_shape=jax.ShapeDtypeStruct((m, n), F32),
        grid=(n // tn,),
        in_specs=[pl.BlockSpec((m, k), lambda j: (0, 0)),
                  pl.BlockSpec((k, tn), lambda j: (0, j)),
                  pl.BlockSpec((1, tn), lambda j: (0, j))],
        out_specs=pl.BlockSpec((m, tn), lambda j: (0, j)),
        compiler_params=_params(("arbitrary",)),
        name="small_mm",
    )(a, w, bias)


ROW_CHUNK = 64


def _modulated_rows(x_ref, gain_ref, sh_ref, sc_ref, row0, n_ctx, emit):
    tm = x_ref.shape[0]

    def body(r, carry):
        r0 = pl.multiple_of(r * ROW_CHUNK, ROW_CHUNK)
        x = x_ref[pl.ds(r0, ROW_CHUNK), :]
        ms = jnp.mean(x * x, axis=-1, keepdims=True)
        y = (x * lax.rsqrt(ms + EPS)) * gain_ref[...]
        rows = row0 + r0 + lax.broadcasted_iota(jnp.int32, (ROW_CHUNK, 1), 0)
        is_ctx = rows < n_ctx
        sc = jnp.where(is_ctx, sc_ref[1:2, :], sc_ref[0:1, :])
        sh = jnp.where(is_ctx, sh_ref[1:2, :], sh_ref[0:1, :])
        emit(r0, y * (1.0 + sc) + sh)
        return carry

    lax.fori_loop(0, tm // ROW_CHUNK, body, 0)


def _rope(a, cos, s1, s2):
    return a * cos + pltpu.roll(a, HEAD_DIM - ROPE_FREQS, 1) * s1 + pltpu.roll(a, ROPE_FREQS, 1) * s2


def _norm_mm_kernel(*refs, n_ctx, tm, tn, mode, n_rope_tiles, n_q_tiles):
    if mode in ("rope", "qknorm_rope"):
        if mode == "qknorm_rope":
            x_ref, gain_ref, sh_ref, sc_ref, w_ref, cos_ref, s1_ref, s2_ref, hg_ref, o_ref, h_scr = refs
        else:
            x_ref, gain_ref, sh_ref, sc_ref, w_ref, cos_ref, s1_ref, s2_ref, o_ref, h_scr = refs
    else:
        x_ref, gain_ref, sh_ref, sc_ref, w_ref, o_ref, h_scr = refs
    i = pl.program_id(0)
    j = pl.program_id(1)

    @pl.when(j == 0)
    def _():
        def emit(r0, h):
            h_scr[pl.ds(r0, ROW_CHUNK), :] = h.astype(BF16)

        _modulated_rows(x_ref, gain_ref, sh_ref, sc_ref, i * tm, n_ctx, emit)

    acc = jnp.dot(h_scr[...], w_ref[...], preferred_element_type=F32)

    if mode == "none":
        o_ref[...] = acc.astype(o_ref.dtype)
    elif mode == "gelu":
        o_ref[...] = jax.nn.gelu(acc).astype(o_ref.dtype)
    else:
        @pl.when(j < n_rope_tiles)
        def _():
            for s in range(tn // HEAD_DIM):
                a = acc[:, s * HEAD_DIM:(s + 1) * HEAD_DIM]
                if mode == "qknorm_rope":
                    ms = jnp.mean(a * a, axis=-1, keepdims=True)
                    a = (a * lax.rsqrt(ms + EPS)) * hg_ref[:, s * HEAD_DIM:(s + 1) * HEAD_DIM]
                a = _rope(a, cos_ref[...], s1_ref[...], s2_ref[...])
                a = a * jnp.where(j < n_q_tiles, QK_FOLD, 1.0)
                o_ref[:, s * HEAD_DIM:(s + 1) * HEAD_DIM] = a.astype(o_ref.dtype)

        @pl.when(j >= n_rope_tiles)
        def _():
            o_ref[...] = acc.astype(o_ref.dtype)


def _norm_mm(xs, gain, sh, sc, w, *, n_ctx, mode, out_dtype, rope=None, head_gain=None, n_rope_cols=0,
             n_q_cols=0):
    nt, d = xs.shape
    n = w.shape[1]
    tm = _pick(nt, (768, 512, 256))
    tn = _pick(math.gcd(math.gcd(n, n_rope_cols), n_q_cols), (512, 256, 128))
    in_specs = [pl.BlockSpec((tm, d), lambda i, j: (i, 0)),
                pl.BlockSpec((1, d), lambda i, j: (0, 0)),
                pl.BlockSpec((2, d), lambda i, j: (0, 0)),
                pl.BlockSpec((2, d), lambda i, j: (0, 0)),
                pl.BlockSpec((d, tn), lambda i, j: (0, j))]
    args = [xs, gain, sh, sc, w]
    if mode in ("rope", "qknorm_rope"):
        assert n_rope_cols % tn == 0
        for t in rope:
            in_specs.append(pl.BlockSpec((tm, HEAD_DIM), lambda i, j: (i, 0)))
            args.append(t)
        if mode == "qknorm_rope":
            in_specs.append(pl.BlockSpec((1, tn), lambda i, j: (0, j)))
            args.append(head_gain)
    return pl.pallas_call(
        functools.partial(_norm_mm_kernel, n_ctx=n_ctx, tm=tm, tn=tn, mode=mode,
                          n_rope_tiles=n_rope_cols // tn, n_q_tiles=n_q_cols // tn),
        out_shape=jax.ShapeDtypeStruct((nt, n), out_dtype),
        grid=(nt // tm, n // tn),
        in_specs=in_specs,
        out_specs=pl.BlockSpec((tm, tn), lambda i, j: (i, j)),
        scratch_shapes=[pltpu.VMEM((tm, d), BF16)],
        compiler_params=_params(("arbitrary", "arbitrary")),
        name="norm_mm_" + mode,
    )(*args)


def _resid_mm_kernel(a_ref, w_ref, r_ref, g_ref, o_ref, *, n_ctx, tm):
    i = pl.program_id(0)
    acc = jnp.dot(a_ref[...], w_ref[...], preferred_element_type=F32)
    rows = i * tm + lax.broadcasted_iota(jnp.int32, (tm, 1), 0)
    g = jnp.where(rows < n_ctx, g_ref[1:2, :], g_ref[0:1, :])
    o_ref[...] = r_ref[...] + g * acc


def _resid_mm(a, w, resid, gate, *, n_ctx):
    nt, k = a.shape
    n = w.shape[1]
    tm = _pick(nt, (1056, 768, 640, 512, 256))
    tn = _pick(n, (512, 256, 128))
    return pl.pallas_call(
        functools.partial(_resid_mm_kernel, n_ctx=n_ctx, tm=tm),
        out_shape=jax.ShapeDtypeStruct((nt, n), F32),
        grid=(nt // tm, n // tn),
        in_specs=[pl.BlockSpec((tm, k), lambda i, j: (i, 0)),
                  pl.BlockSpec((k, tn), lambda i, j: (0, j)),
                  pl.BlockSpec((tm, tn), lambda i, j: (i, j)),
                  pl.BlockSpec((2, tn), lambda i, j: (0, j))],
        out_specs=pl.BlockSpec((tm, tn), lambda i, j: (i, j)),
        compiler_params=_params(("arbitrary", "arbitrary")),
        name="resid_mm",
    )(a, w, resid, gate)


QK_FOLD = HEAD_DIM ** -0.5 * math.log2(math.e)
_NT_DIMS = (((1,), (1,)), ((), ()))


def _stack_heads(q_ref, r0, tq):
    return jnp.concatenate([q_ref[r0:r0 + tq, g * HEAD_DIM:(g + 1) * HEAD_DIM] for g in range(GROUP)], axis=0)


def _with_ones(v):
    return jnp.concatenate([v, jnp.ones(v.shape, v.dtype)], axis=1)


def _attn_win_kernel(sink_ref, q_ref, k_ref, v_ref, o_ref, s_buf, *, tq, nq, n_ctx, wk):
    h = pl.program_id(0)
    step = pl.program_id(1)
    nt = k_ref.shape[0]
    n_ctx_tiles = n_ctx // tq

    def window(t):
        tile = step * nq + t
        start = pl.multiple_of(jnp.clip(tile * tq - WINDOW, 0, nt - wk), WINDOW)
        return tile, start

    def s_tile(t):
        _, start = window(t)
        k_all = jnp.concatenate([k_ref[0:n_ctx, :], k_ref[pl.ds(start, wk), :]], axis=0)
        s_buf[t % 2] = lax.dot_general(_stack_heads(q_ref, t * tq, tq), k_all, _NT_DIMS,
                                       preferred_element_type=F32)

    def consume(t):
        tile, start = window(t)
        v_aug = _with_ones(jnp.concatenate([v_ref[0:n_ctx, :], v_ref[pl.ds(start, wk), :]], axis=0))
        c = lax.broadcasted_iota(jnp.int32, (tq, wk), 1)
        r = lax.broadcasted_iota(jnp.int32, (tq, wk), 0)
        rel = (c - r) + ((start - n_ctx) - (tile - n_ctx_tiles) * tq)
        reach = jnp.where(tile >= n_ctx_tiles, WINDOW, -1)
        bias = jnp.where(jnp.abs(rel) <= reach, jnp.where(c >= n_ctx - start, 0.0, NEG), NEG)
        bias = jnp.concatenate([jnp.zeros((tq, n_ctx), F32), bias], axis=1)
        for g in range(GROUP):
            s = s_buf[t % 2, g * tq:(g + 1) * tq, :] + bias
            sk = sink_ref[h * GROUP + g] * math.log2(math.e)
            m = jnp.maximum(jnp.max(s, axis=1, keepdims=True), sk)
            p = jnp.exp2(s - m)
            acc = jnp.dot(p.astype(BF16), v_aug, preferred_element_type=F32)
            o = acc[:, :HEAD_DIM] / (acc[:, HEAD_DIM:] + jnp.exp2(sk - m))
            o_ref[t * tq:(t + 1) * tq, g * HEAD_DIM:(g + 1) * HEAD_DIM] = o.astype(o_ref.dtype)

    s_tile(0)
    for t in range(nq):
        if t + 1 < nq:
            s_tile(t + 1)
        consume(t)


def _attn_win(qkv, sink, *, n_ctx, n_heads, n_kv):
    nt = qkv.shape[0]
    tq = n_ctx
    wk = tq + 2 * WINDOW
    assert tq % WINDOW == 0 and nt % tq == 0 and nt >= wk
    nq = _pick(nt // tq, (3, 2, 1))
    kcol, vcol = n_heads, n_heads + n_kv
    return pl.pallas_call(
        functools.partial(_attn_win_kernel, tq=tq, nq=nq, n_ctx=n_ctx, wk=wk),
        out_shape=jax.ShapeDtypeStruct((nt, n_heads * HEAD_DIM), BF16),
        grid=(n_kv, nt // (nq * tq)),
        in_specs=[pl.BlockSpec(memory_space=pltpu.SMEM),
                  pl.BlockSpec((nq * tq, GROUP * HEAD_DIM), lambda h, i: (i, h)),
                  pl.BlockSpec((nt, HEAD_DIM), lambda h, i: (0, kcol + h)),
                  pl.BlockSpec((nt, HEAD_DIM), lambda h, i: (0, vcol + h))],
        out_specs=pl.BlockSpec((nq * tq, GROUP * HEAD_DIM), lambda h, i: (i, h)),
        scratch_shapes=[pltpu.VMEM((2, GROUP * tq, n_ctx + wk), F32)],
        compiler_params=_params(("arbitrary", "arbitrary")),
        name="attn_win",
    )(sink, qkv, qkv, qkv)


def _attn_full_kernel(q_ref, k_ref, v_ref, o_ref, s_buf, m_scr, acc_scr, *, tq, tk, n_ctx):
    i = pl.program_id(1)
    nt = k_ref.shape[0]
    n_chunks = nt // tk

    def s_tile(c, slot):
        off = pl.multiple_of(c * tk, tk)
        s_buf[slot] = lax.dot_general(_stack_heads(q_ref, 0, tq), k_ref[pl.ds(off, tk), :], _NT_DIMS,
                                      preferred_element_type=F32)

    def consume(c, slot):
        off = pl.multiple_of(c * tk, tk)
        v_aug = _with_ones(v_ref[pl.ds(off, tk), :])
        for g in range(GROUP):
            rows = slice(g * tq, (g + 1) * tq)
            s = s_buf[slot, rows, :]
            m_prev = m_scr[rows, :]
            m_new = jnp.maximum(m_prev, jnp.max(s, axis=1, keepdims=True))
            p = jnp.exp2(s - m_new)
            acc_scr[rows, :] = (jnp.exp2(m_prev - m_new) * acc_scr[rows, :]
                                + jnp.dot(p.astype(BF16), v_aug, preferred_element_type=F32))
            m_scr[rows, :] = m_new

    def finish():
        acc = acc_scr[...]
        o = acc[:, :HEAD_DIM] / acc[:, HEAD_DIM:]
        for g in range(GROUP):
            o_ref[:, g * HEAD_DIM:(g + 1) * HEAD_DIM] = o[g * tq:(g + 1) * tq].astype(o_ref.dtype)

    @pl.when(i < n_ctx // tq)
    def _():
        s = lax.dot_general(_stack_heads(q_ref, 0, tq), k_ref[0:n_ctx, :], _NT_DIMS, preferred_element_type=F32)
        p = jnp.exp2(s - jnp.max(s, axis=1, keepdims=True))
        acc_scr[...] = jnp.dot(p.astype(BF16), _with_ones(v_ref[0:n_ctx, :]), preferred_element_type=F32)
        finish()

    @pl.when(i >= n_ctx // tq)
    def _():
        m_scr[...] = jnp.full(m_scr.shape, NEG, F32)
        acc_scr[...] = jnp.zeros(acc_scr.shape, F32)
        s_tile(0, 0)

        def pair(j, carry):
            s_tile(2 * j + 1, 1)
            consume(2 * j, 0)
            s_tile(2 * j + 2, 0)
            consume(2 * j + 1, 1)
            return carry

        lax.fori_loop(0, (n_chunks - 1) // 2, pair, 0)
        if n_chunks % 2 == 1:
            consume(n_chunks - 1, 0)
        else:
            s_tile(n_chunks - 1, 1)
            consume(n_chunks - 2, 0)
            consume(n_chunks - 1, 1)
        finish()


def _attn_full(qkv, *, n_ctx, n_heads, n_kv):
    nt = qkv.shape[0]
    tq = _pick(n_ctx, (256, 128))
    tk = _pick(nt, (768, 512, 384, 256, 128))
    assert nt % tq == 0 and n_ctx % tq == 0
    kcol, vcol = n_heads, n_heads + n_kv
    return pl.pallas_call(
        functools.partial(_attn_full_kernel, tq=tq, tk=tk, n_ctx=n_ctx),
        out_shape=jax.ShapeDtypeStruct((nt, n_heads * HEAD_DIM), BF16),
        grid=(n_kv, nt // tq),
        in_specs=[pl.BlockSpec((tq, GROUP * HEAD_DIM), lambda h, i: (i, h)),
                  pl.BlockSpec((nt, HEAD_DIM), lambda h, i: (0, kcol + h)),
                  pl.BlockSpec((nt, HEAD_DIM), lambda h, i: (0, vcol + h))],
        out_specs=pl.BlockSpec((tq, GROUP * HEAD_DIM), lambda h, i: (i, h)),
        scratch_shapes=[pltpu.VMEM((2, GROUP * tq, tk), F32), pltpu.VMEM((GROUP * tq, 1), F32),
                        pltpu.VMEM((GROUP * tq, 2 * HEAD_DIM), F32)],
        compiler_params=_params(("arbitrary", "arbitrary")),
        name="attn_full",
    )(qkv, qkv, qkv)


def _rglru_kernel(u_ref, gate_ref, cw_ref, cb_ref, wr_ref, wi_ref, br_ref, bi_ref, lam_ref, o_ref, rec_scr,
                  *, tc, n_chunks, n_ctx_chunks):
    nt, ch = u_ref.shape
    row = lax.broadcasted_iota(jnp.int32, (tc, 1), 0)

    def conv_chunk(c):
        t0 = pl.multiple_of(c * tc, tc)
        x0 = u_ref[pl.ds(t0, tc), :]
        seg_start = (c == 0) | (c == n_ctx_chunks)
        seg_end = (c == n_ctx_chunks - 1) | (c == n_chunks - 1)
        p0 = pl.multiple_of(jnp.maximum(t0 - 8, 0), 8)
        n0 = pl.multiple_of(jnp.minimum(t0 + tc, nt - 8), 8)
        prev = u_ref[pl.ds(p0, 8), :] * jnp.where(seg_start, 0.0, 1.0)
        nxt = u_ref[pl.ds(n0, 8), :] * jnp.where(seg_end, 0.0, 1.0)
        xm1 = jnp.where(row == 0, prev[7:8, :], pltpu.roll(x0, 1, 0))
        xm2 = jnp.where(row == 0, prev[6:7, :], jnp.where(row == 1, prev[7:8, :], pltpu.roll(x0, 2, 0)))
        xp1 = jnp.where(row == tc - 1, nxt[0:1, :], pltpu.roll(x0, tc - 1, 0))
        uc = cw_ref[0:1, :] * xm2 + cw_ref[1:2, :] * xm1 + cw_ref[2:3, :] * x0 + cw_ref[3:4, :] * xp1
        return t0, uc + cb_ref[...]

    def coeffs(uc, d):
        ub = uc.astype(BF16)
        r = jax.nn.sigmoid(jnp.dot(ub, wr_ref[d, 0], preferred_element_type=F32) + br_ref[d:d + 1, :])
        gi = jax.nn.sigmoid(jnp.dot(ub, wi_ref[d, 0], preferred_element_type=F32) + bi_ref[d:d + 1, :])
        x = -lam_ref[d:d + 1, :]
        softplus = jnp.maximum(x, 0.0) + jnp.log1p(jnp.exp(-jnp.abs(x)))
        log_a = (-LRU_C * r) * softplus
        a = jnp.exp(log_a)
        one_minus_a2 = -jnp.tanh(log_a) * (a * a + 1.0)
        return a, jnp.sqrt(one_minus_a2) * (gi * uc)

    def scan_chunk(a, b, h0, reverse):
        first = tc - 1 if reverse else 0
        b = jnp.where(row == first, b + a * h0, b)
        s = 1
        while s < tc:
            if reverse:
                keep = row < tc - s
                a_sh, b_sh = pltpu.roll(a, tc - s, 0), pltpu.roll(b, tc - s, 0)
            else:
                keep = row >= s
                a_sh, b_sh = pltpu.roll(a, s, 0), pltpu.roll(b, s, 0)
            b = a * jnp.where(keep, b_sh, 0.0) + b
            a = a * jnp.where(keep, a_sh, 1.0)
            s *= 2
        return b

    def fwd(c, h):
        t0, uc = conv_chunk(c)
        a, b = coeffs(uc, 0)
        hh = scan_chunk(a, b, h, False)
        rec_scr[pl.ds(t0, tc), :] = hh
        return hh[tc - 1:tc, :]

    def bwd(j, h):
        c = jnp.where(j < n_ctx_chunks, n_ctx_chunks - 1 - j, n_chunks - 1 - (j - n_ctx_chunks))
        t0, uc = conv_chunk(c)
        a, b = coeffs(uc, 1)
        hh = scan_chunk(a, b, h, True)
        rec = rec_scr[pl.ds(t0, tc), :] + hh
        o_ref[pl.ds(t0, tc), :] = (gate_ref[pl.ds(t0, tc), :].astype(F32) * rec).astype(o_ref.dtype)
        return hh[0:1, :]

    zero = jnp.zeros((1, ch), F32)
    lax.fori_loop(0, n_chunks, fwd, zero)
    lax.fori_loop(0, n_chunks, bwd, zero)


def _rglru(u, gate, conv_w, conv_b, w_ra, w_ix, b_ra, b_ix, lam, *, n_ctx):
    nt, d = u.shape
    nb, ch = w_ra.shape[1], w_ra.shape[2]
    tc = _pick(n_ctx, (256, 128))
    assert n_ctx % tc == 0 and nt % tc == 0
    col = lambda h: (0, h)
    return pl.pallas_call(
        functools.partial(_rglru_kernel, tc=tc, n_chunks=nt // tc, n_ctx_chunks=n_ctx // tc),
        out_shape=jax.ShapeDtypeStruct((nt, d), BF16),
        grid=(nb,),
        in_specs=[pl.BlockSpec((nt, ch), col),
                  pl.BlockSpec((nt, ch), col),
                  pl.BlockSpec((conv_w.shape[0], ch), col),
                  pl.BlockSpec((1, ch), col),
                  pl.BlockSpec((2, 1, ch, ch), lambda h: (0, h, 0, 0)),
                  pl.BlockSpec((2, 1, ch, ch), lambda h: (0, h, 0, 0)),
                  pl.BlockSpec((2, ch), col),
                  pl.BlockSpec((2, ch), col),
                  pl.BlockSpec((2, ch), col)],
        out_specs=pl.BlockSpec((nt, ch), col),
        scratch_shapes=[pltpu.VMEM((nt, ch), F32)],
        compiler_params=_params(("arbitrary",)),
        name="rglru",
    )(u, gate, conv_w, conv_b, w_ra, w_ix, b_ra, b_ix, lam)


def _ffn_pre_kernel(x_ref, gain_ref, sh_ref, sc_ref, wr_ref, hp_ref, aff_ref, *, n_ctx, tm, n_experts):
    i = pl.program_id(0)
    half = x_ref.shape[1] // 2

    def emit(r0, h):
        hb = h.astype(BF16)
        bits = pltpu.bitcast(hb.astype(F32), jnp.uint32)
        hp_ref[pl.ds(r0, ROW_CHUNK), :] = (bits[:, :half] & jnp.uint32(0xFFFF0000)) | (bits[:, half:] >> 16)
        logits = jnp.dot(hb, wr_ref[...], preferred_element_type=F32)
        lane = lax.broadcasted_iota(jnp.int32, logits.shape, 1)
        logits = jnp.where(lane < n_experts, logits, NEG)
        e = jnp.exp(logits - jnp.max(logits, axis=-1, keepdims=True))
        aff_ref[pl.ds(r0, ROW_CHUNK), :] = e / jnp.sum(e, axis=-1, keepdims=True)

    _modulated_rows(x_ref, gain_ref, sh_ref, sc_ref, i * tm, n_ctx, emit)


def _ffn_pre(xs, gain, sh, sc, w_router_pad, *, n_ctx, n_experts):
    nt, d = xs.shape
    tm = _pick(nt, (768, 512, 256))
    return pl.pallas_call(
        functools.partial(_ffn_pre_kernel, n_ctx=n_ctx, tm=tm, n_experts=n_experts),
        out_shape=(jax.ShapeDtypeStruct((nt, d // 2), jnp.uint32),
                   jax.ShapeDtypeStruct((nt, LANES), F32)),
        grid=(nt // tm,),
        in_specs=[pl.BlockSpec((tm, d), lambda i: (i, 0)),
                  pl.BlockSpec((1, d), lambda i: (0, 0)),
                  pl.BlockSpec((2, d), lambda i: (0, 0)),
                  pl.BlockSpec((2, d), lambda i: (0, 0)),
                  pl.BlockSpec((d, LANES), lambda i: (0, 0))],
        out_specs=(pl.BlockSpec((tm, d // 2), lambda i: (i, 0)),
                   pl.BlockSpec((tm, LANES), lambda i: (i, 0))),
        compiler_params=_params(("arbitrary",)),
        name="ffn_pre",
    )(xs, gain, sh, sc, w_router_pad)


def _pack_bf16_pair(hi, lo):
    hi_bits = pltpu.bitcast(hi.astype(BF16).astype(F32), jnp.uint32)
    lo_bits = pltpu.bitcast(lo.astype(BF16).astype(F32), jnp.uint32)
    return (hi_bits & jnp.uint32(0xFFFF0000)) | (lo_bits >> 16)


def _unpack_bf16_pair(w):
    return (pltpu.bitcast(w & jnp.uint32(0xFFFF0000), F32).astype(BF16),
            pltpu.bitcast(w << 16, F32).astype(BF16))


def _moe_kernel(idx_ref, hp_hbm, g_ref, wg_ref, wu_ref, wda_ref, wdb_ref, o_ref, land, xb, act, sem,
                *, nf, tf, per):
    e = pl.program_id(0)
    s = pl.program_id(1)
    n_experts = pl.num_programs(0)
    n_steps = pl.num_programs(1)
    rows, half = xb.shape[0], land.shape[1]

    def gather_row(expert, p):
        src = idx_ref[expert, p]
        pltpu.make_async_copy(hp_hbm.at[pl.ds(src, 1), :], land.at[pl.ds(p, 1), :], sem).start()

    def wait_rows():
        pltpu.make_async_copy(hp_hbm.at[pl.ds(0, land.shape[0]), :], land, sem).wait()

    def gather_ahead():
        nxt = lax.rem(e + 1, n_experts)
        for k in range(per):
            gather_row(nxt, s * per + k)

    @pl.when((e == 0) & (s == 0))
    def _():
        def body(p, carry):
            gather_row(0, p)
            return carry

        lax.fori_loop(0, land.shape[0], body, 0)

    @pl.when(s == 0)
    def _():
        wait_rows()
        xb[:, :half], xb[:, half:] = _unpack_bf16_pair(land[0:rows, :])

    @pl.when(s < nf)
    def _():
        gather_ahead()
        x = xb[...]
        a = jnp.dot(x, wg_ref[0, 0].astype(BF16), preferred_element_type=F32)
        u = jnp.dot(x, wu_ref[0, 0].astype(BF16), preferred_element_type=F32)
        act[s] = (a * jax.nn.sigmoid(a) * u).astype(BF16)

    @pl.when(s >= nf)
    def _():
        gather_ahead()

        def down(wd_ref):
            acc = jnp.dot(act[0], wd_ref[0, 0, 0:tf, :].astype(BF16), preferred_element_type=F32)
            for f in range(1, nf):
                acc = acc + jnp.dot(act[f], wd_ref[0, 0, f * tf:(f + 1) * tf, :].astype(BF16),
                                    preferred_element_type=F32)
            return acc * g_ref[0]

        o_ref[0] = _pack_bf16_pair(down(wda_ref), down(wdb_ref))

    @pl.when((e == n_experts - 1) & (s == n_steps - 1))
    def _():
        wait_rows()


def _moe(idx, hp, gates, w_gate, w_up, w_down, layer):
    n_experts, rows = idx.shape
    half = hp.shape[1]
    dexp = w_gate.shape[3]
    tf = _pick(dexp, (256, 128))
    tp = _pick(half, (512, 256, 128))
    nf, nd = dexp // tf, half // tp
    per = -(-rows // ((nf + nd) * SUBLANES)) * SUBLANES
    rows_pad = per * (nf + nd)
    idx = jnp.concatenate([idx, jnp.broadcast_to(idx[:, -1:], (n_experts, rows_pad - rows))], axis=1)
    up_map = lambda e, s, idx: (layer, e, 0, jnp.minimum(s, nf - 1))
    grid_spec = pltpu.PrefetchScalarGridSpec(
        num_scalar_prefetch=1,
        grid=(n_experts, nf + nd),
        in_specs=[pl.BlockSpec(memory_space=pl.ANY),
                  pl.BlockSpec((1, rows, 1), lambda e, s, idx: (e, 0, 0)),
                  pl.BlockSpec((1, 1, 2 * half, tf), up_map),
                  pl.BlockSpec((1, 1, 2 * half, tf), up_map),
                  pl.BlockSpec((1, 1, dexp, tp), lambda e, s, idx: (layer, e, 0, jnp.maximum(s - nf, 0))),
                  pl.BlockSpec((1, 1, dexp, tp), lambda e, s, idx: (layer, e, 0, jnp.maximum(s - nf, 0) + nd))],
        out_specs=pl.BlockSpec((1, rows, tp), lambda e, s, idx: (e, 0, jnp.maximum(s - nf, 0))),
        scratch_shapes=[pltpu.VMEM((rows_pad, half), jnp.uint32),
                        pltpu.VMEM((rows, 2 * half), BF16),
                        pltpu.VMEM((nf, rows, tf), BF16),
                        pltpu.SemaphoreType.DMA(())],
    )
    return pl.pallas_call(
        functools.partial(_moe_kernel, nf=nf, tf=tf, per=per),
        out_shape=jax.ShapeDtypeStruct((n_experts, rows, half), jnp.uint32),
        grid_spec=grid_spec,
        compiler_params=_params(("arbitrary", "arbitrary")),
        name="moe_experts",
    )(idx, hp, gates, w_gate, w_up, w_down, w_down)


COMBINE_TB = 256
COMBINE_W = 64


def _combine_kernel(src_ref, nwin_ref, y_hbm, slots_ref, pos_ref, rep_ref, x_ref, g_ref, o_ref,
                    ybuf, yextra, sem, sem_extra, *, n_experts, rows, n_ctx):
    b = pl.program_id(0)
    n_blocks = pl.num_programs(0)
    tb, d = x_ref.shape
    half = d // 2
    n_windows = slots_ref.shape[1]
    w = COMBINE_W
    slot = lax.rem(b, 2)

    q = pos_ref[...] + 1
    q_hi = (q >> 5).astype(F32).astype(BF16)
    q_lo = (q & 31).astype(F32).astype(BF16)
    pos_wide = (32.0 * jnp.dot(q_hi, rep_ref[...], preferred_element_type=F32)
                + jnp.dot(q_lo, rep_ref[...], preferred_element_type=F32) - 1.0)

    def fetch(block, window, dst, dma_sem):
        for e in range(n_experts):
            src = pl.multiple_of(e * rows + src_ref[(block * n_windows + window) * n_experts + e], SUBLANES)
            pltpu.make_async_copy(y_hbm.at[pl.ds(src, w), :], dst.at[pl.ds(e * w, w), :], dma_sem).start()

    def wait(dst, dma_sem):
        pltpu.make_async_copy(y_hbm.at[pl.ds(0, n_experts * w), :], dst, dma_sem).wait()

    def expand(buf, window, first):
        window_slots = slots_ref[0, pl.ds(window, 1), :].astype(F32)
        onehot = jnp.where(pos_wide == window_slots, 1.0, 0.0).astype(BF16)
        chunk = _pick(half, (512, 256, 128))
        for k in range(half // chunk):
            cols = slice(k * chunk, (k + 1) * chunk)
            cols_hi = slice(half + k * chunk, half + (k + 1) * chunk)
            hi, lo = _unpack_bf16_pair(buf[:, cols])
            a_hi = jnp.dot(onehot, hi, preferred_element_type=F32)
            a_lo = jnp.dot(onehot, lo, preferred_element_type=F32)
            if first:
                o_ref[:, cols] = a_hi
                o_ref[:, cols_hi] = a_lo
            else:
                o_ref[:, cols] += a_hi
                o_ref[:, cols_hi] += a_lo

    @pl.when(b == 0)
    def _():
        fetch(0, 0, ybuf.at[0], sem.at[0])

    wait(ybuf.at[slot], sem.at[slot])

    @pl.when(b + 1 < n_blocks)
    def _():
        fetch(b + 1, 0, ybuf.at[1 - slot], sem.at[1 - slot])

    expand(ybuf.at[slot], 0, True)

    def extra(window, carry):
        fetch(b, window, yextra, sem_extra)
        wait(yextra, sem_extra)
        expand(yextra, window, False)
        return carry

    lax.fori_loop(1, nwin_ref[b], extra, 0)

    row = b * tb + lax.broadcasted_iota(jnp.int32, (tb, 1), 0)
    g = jnp.where(row < n_ctx, g_ref[1:2, :], g_ref[0:1, :])
    o_ref[...] = x_ref[...] + g * o_ref[...]


def _combine_plan(rows_sorted, nt):
    n_experts, r = rows_sorted.shape
    tb, w = COMBINE_TB, COMBINE_W
    served = w - SUBLANES
    nb, nw = nt // tb, -(-tb // served)
    tok = jnp.arange(nt, dtype=jnp.int32)[:, None, None]
    below = jnp.sum((rows_sorted[None] < tok).astype(jnp.int32), axis=2)
    picked = jnp.sum((rows_sorted[None] == tok).astype(jnp.int32), axis=2) > 0
    pos = jnp.pad(jnp.where(picked, below, -1), ((0, 0), (0, LANES - n_experts)), constant_values=-1)
    start = below[::tb]
    stop = jnp.concatenate([start[1:], jnp.full((1, n_experts), r, jnp.int32)], axis=0)
    first = start[:, None, :] + jnp.arange(nw, dtype=jnp.int32)[None, :, None] * served
    src = jnp.minimum(first // SUBLANES * SUBLANES, r - w)
    slot = src[..., None] + jnp.arange(w, dtype=jnp.int32)
    last = jnp.minimum(first + served, stop[:, None, :])
    valid = (slot >= first[..., None]) & (slot < last[..., None])
    slots = jnp.where(valid, slot, -2).reshape(nb, nw, n_experts * w)
    n_win = jnp.maximum(1, jnp.max((stop - start + served - 1) // served, axis=1)).astype(jnp.int32)
    lane_expert = jnp.arange(n_experts * w, dtype=jnp.int32) // w
    rep = (jnp.arange(LANES, dtype=jnp.int32)[:, None] == lane_expert[None, :]).astype(BF16)
    return src.reshape(-1), n_win, slots, pos, rep


def _combine(y_packed, rows_sorted, xs, gate, *, n_ctx):
    nt, d = xs.shape
    n_experts, r, half = y_packed.shape
    tb = COMBINE_TB
    assert nt % tb == 0 and r >= COMBINE_W and r % SUBLANES == 0 and half % LANES == 0
    src, n_win, slots, pos, rep = _combine_plan(rows_sorted, nt)
    nw = slots.shape[1]
    grid_spec = pltpu.PrefetchScalarGridSpec(
        num_scalar_prefetch=2,
        grid=(nt // tb,),
        in_specs=[pl.BlockSpec(memory_space=pl.ANY),
                  pl.BlockSpec((1, nw, n_experts * COMBINE_W), lambda b, src, nwin: (b, 0, 0)),
                  pl.BlockSpec((tb, LANES), lambda b, src, nwin: (b, 0)),
                  pl.BlockSpec((LANES, n_experts * COMBINE_W), lambda b, src, nwin: (0, 0)),
                  pl.BlockSpec((tb, d), lambda b, src, nwin: (b, 0)),
                  pl.BlockSpec((2, d), lambda b, src, nwin: (0, 0))],
        out_specs=pl.BlockSpec((tb, d), lambda b, src, nwin: (b, 0)),
        scratch_shapes=[pltpu.VMEM((2, n_experts * COMBINE_W, half), jnp.uint32),
                        pltpu.VMEM((n_experts * COMBINE_W, half), jnp.uint32),
                        pltpu.SemaphoreType.DMA((2,)),
                        pltpu.SemaphoreType.DMA(())],
    )
    return pl.pallas_call(
        functools.partial(_combine_kernel, n_experts=n_experts, rows=r, n_ctx=n_ctx),
        out_shape=jax.ShapeDtypeStruct((nt, d), F32),
        grid_spec=grid_spec,
        compiler_params=_params(("arbitrary",)),
        name="moe_combine",
    )(src, n_win, y_packed.reshape(n_experts * r, half), slots, pos, rep, xs, gate)


def _final_norm_kernel(x_ref, gain_ref, o_ref):
    x = x_ref[...]
    ms = jnp.mean(x * x, axis=-1, keepdims=True)
    o_ref[...] = (x * lax.rsqrt(ms + EPS)) * gain_ref[...]


def _final_norm(xs, gain, *, n_ctx):
    nt, d = xs.shape
    tm = _pick(n_ctx, (256, 128))
    off = n_ctx // tm
    return pl.pallas_call(
        _final_norm_kernel,
        out_shape=jax.ShapeDtypeStruct((nt - n_ctx, d), F32),
        grid=((nt - n_ctx) // tm,),
        in_specs=[pl.BlockSpec((tm, d), lambda i: (i + off, 0)),
                  pl.BlockSpec((1, d), lambda i: (0, 0))],
        out_specs=pl.BlockSpec((tm, d), lambda i: (i, 0)),
        compiler_params=_params(("arbitrary",)),
        name="final_norm",
    )(xs, gain)


def _rope_tables(seq, n_ctx):
    rows = seq // GRID_W
    row = jnp.repeat(jnp.arange(rows), GRID_W).astype(F32)
    col = jnp.tile(jnp.arange(GRID_W), rows).astype(F32)
    inv_freq = ROPE_THETA ** (-jnp.arange(ROPE_FREQS, dtype=F32) / ROPE_FREQS)
    ang_r = row[:, None] * inv_freq
    ang_c = col[:, None] * inv_freq
    zero = jnp.zeros_like(ang_r)
    cos = jnp.concatenate([jnp.cos(ang_r), jnp.cos(ang_r), jnp.cos(ang_c), jnp.cos(ang_c)], axis=1)
    s1 = jnp.concatenate([-jnp.sin(ang_r), zero, -jnp.sin(ang_c), zero], axis=1)
    s2 = jnp.concatenate([zero, jnp.sin(ang_r), zero, jnp.sin(ang_c)], axis=1)
    pad = lambda t, v: jnp.concatenate([jnp.full((n_ctx, HEAD_DIM), v, F32), t], axis=0)
    return pad(cos, 1.0), pad(s1, 0.0), pad(s2, 0.0)


def _route(aff, n_ctx, n_experts):
    nt = aff.shape[0]
    a_ctx = aff[:n_ctx, :n_experts].T
    a_lat = aff[n_ctx:, :n_experts].T
    g_l, i_l = lax.top_k(a_lat, CAPACITY_FACTOR * (nt - n_ctx) // n_experts)
    g_c, i_c = lax.top_k(a_ctx, CAPACITY_FACTOR * n_ctx // n_experts)
    idx = jnp.concatenate([i_l + n_ctx, i_c], axis=1).astype(jnp.int32)
    gates = jnp.concatenate([g_l, g_c], axis=1)
    return lax.sort((idx, gates), dimension=1, num_keys=1)


def kernel(x, c, ctx, c_ctx, mod_down, mod_up, mod_bias, norm_mix, norm_ffn, norm_final, a_w_qkv, a_w_o, a_sink, b_w_qkv, b_w_o, b_q_norm, b_k_norm, c_w_in, c_conv_w, c_conv_b, c_w_ra, c_b_ra, c_w_ix, c_b_ix, c_lambda, c_w_out, moe_router, moe_w_gate, moe_w_up, moe_w_down):
    assert x.shape[0] == 1 and ctx.shape[0] == 1
    seq, d = x.shape[1], x.shape[2]
    n_ctx = ctx.shape[1]
    depth = mod_down.shape[0]
    n_experts = moe_router.shape[2]
    n_heads = a_w_o.shape[1] // HEAD_DIM
    n_kv = n_heads // GROUP
    d_rnn = c_w_out.shape[1]

    xs = jnp.concatenate([ctx[0], x[0]], axis=0)
    cond = jnp.concatenate([c, c_ctx[None, :], jnp.zeros((6, d), F32)], axis=0)
    rope = _rope_tables(seq, n_ctx)
    zero_bias = jnp.zeros((1, mod_down.shape[2]), F32)

    for i in range(depth):
        low = _small_mm(cond, mod_down[i], zero_bias, silu=True)
        mod = _small_mm(low, mod_up[i], mod_bias[i][None, :], silu=False)[:2].reshape(2, 6, d)
        sh1, sc1, g1, sh2, sc2, g2 = (mod[:, k, :] for k in range(6))
        gain_mix = norm_mix[i][None, :]
        kind, j = i % 3, i // 3
        if kind == 0:
            qkv = _norm_mm(xs, gain_mix, sh1, sc1, a_w_qkv[j].astype(BF16), n_ctx=n_ctx, mode="rope",
                           out_dtype=BF16, rope=rope, n_rope_cols=(n_heads + n_kv) * HEAD_DIM,
                           n_q_cols=n_heads * HEAD_DIM)
            y = _attn_win(qkv, a_sink[j], n_ctx=n_ctx, n_heads=n_heads, n_kv=n_kv)
            w_out = a_w_o[j]
        elif kind == 1:
            head_gain = jnp.concatenate([jnp.tile(b_q_norm[j], n_heads), jnp.tile(b_k_norm[j], n_kv),
                                         jnp.ones((n_kv * HEAD_DIM,), F32)])[None, :]
            qkv = _norm_mm(xs, gain_mix, sh1, sc1, b_w_qkv[j].astype(BF16), n_ctx=n_ctx, mode="qknorm_rope",
                           out_dtype=BF16, rope=rope, head_gain=head_gain,
                           n_rope_cols=(n_heads + n_kv) * HEAD_DIM, n_q_cols=n_heads * HEAD_DIM)
            y = _attn_full(qkv, n_ctx=n_ctx, n_heads=n_heads, n_kv=n_kv)
            w_out = b_w_o[j]
        else:
            gate = _norm_mm(xs, gain_mix, sh1, sc1, c_w_in[j][:, :d_rnn].astype(BF16), n_ctx=n_ctx,
                            mode="gelu", out_dtype=BF16)
            u = _norm_mm(xs, gain_mix, sh1, sc1, c_w_in[j][:, d_rnn:].astype(BF16), n_ctx=n_ctx,
                         mode="none", out_dtype=F32)
            y = _rglru(u, gate, c_conv_w[j], c_conv_b[j][None, :], c_w_ra[j].astype(BF16),
                       c_w_ix[j].astype(BF16), c_b_ra[j], c_b_ix[j], c_lambda[j], n_ctx=n_ctx)
            w_out = c_w_out[j]
        xs = _resid_mm(y, w_out.astype(BF16), xs, g1, n_ctx=n_ctx)

        w_router = jnp.pad(moe_router[i], ((0, 0), (0, LANES - n_experts))).astype(BF16)
        hp, aff = _ffn_pre(xs, norm_ffn[i][None, :], sh2, sc2, w_router, n_ctx=n_ctx, n_experts=n_experts)
        idx, gates = _route(aff, n_ctx, n_experts)
        yexp = _moe(idx, hp, gates[:, :, None], moe_w_gate, moe_w_up, moe_w_down, i)
        xs = _combine(yexp, idx, xs, g2, n_ctx=n_ctx)

    return _final_norm(xs, norm_final[None, :], n_ctx=n_ctx)[None]
```

```python
import functools
import math

import jax
import jax.numpy as jnp
from jax import lax
from jax.experimental import pallas as pl
from jax.experimental.pallas import tpu as pltpu

F32 = jnp.float32
BF16 = jnp.bfloat16

HEAD_DIM = 128
GROUP = 4
WINDOW = 128
GRID_W = 64
ROPE_THETA = 10000.0
ROPE_FREQS = HEAD_DIM // 4
CAPACITY_FACTOR = 2
LRU_C = 8.0
EPS = 1e-6
NEG = -1e30

V7X_VMEM_BYTES = 64 * 1024 * 1024
VMEM_LIMIT = V7X_VMEM_BYTES - 8 * 1024 * 1024
LANES = 128
SUBLANES = 8


def _pick(n, candidates):
    for c in candidates:
        if c <= n and n % c == 0:
            return c
    return n


def _params(sem):
    return pltpu.CompilerParams(dimension_semantics=sem, vmem_limit_bytes=VMEM_LIMIT)


def _small_mm_kernel(a_ref, w_ref, b_ref, o_ref, *, silu):
    a = a_ref[...]
    if silu:
        a = a * jax.nn.sigmoid(a)
    acc = jnp.dot(a.astype(BF16), w_ref[...].astype(BF16), preferred_element_type=F32)
    o_ref[...] = acc + b_ref[...]


def _small_mm(a, w, bias, *, silu):
    m, k = a.shape
    n = w.shape[1]
    tn = _pick(n, (2048, 1024, 512, 256, 128))
    return pl.pallas_call(
        functools.partial(_small_mm_kernel, silu=silu),
        out_shape=jax.ShapeDtypeStruct((m, n), F32),
        grid=(n // tn,),
        in_specs=[pl.BlockSpec((m, k), lambda j: (0, 0)),
                  pl.BlockSpec((k, tn), lambda j: (0, j)),
                  pl.BlockSpec((1, tn), lambda j: (0, j))],
        out_specs=pl.BlockSpec((m, tn), lambda j: (0, j)),
        compiler_params=_params(("arbitrary",)),
        name="small_mm",
    )(a, w, bias)


ROW_CHUNK = 64


def _modulated_rows(x_ref, gain_ref, sh_ref, sc_ref, rs_scr, row0, n_ctx, emit):
    tm = x_ref.shape[0]

    def scales(r, carry):
        r0 = pl.multiple_of(r * ROW_CHUNK, ROW_CHUNK)
        x = x_ref[pl.ds(r0, ROW_CHUNK), :]
        rs_scr[pl.ds(r0, ROW_CHUNK), :] = lax.rsqrt(jnp.mean(x * x, axis=-1, keepdims=True) + EPS)
        return carry

    def body(r, carry):
        r0 = pl.multiple_of(r * ROW_CHUNK, ROW_CHUNK)
        y = (x_ref[pl.ds(r0, ROW_CHUNK), :] * rs_scr[pl.ds(r0, ROW_CHUNK), :]) * gain_ref[...]
        is_ctx = row0 + r0 < n_ctx
        sc = jnp.where(is_ctx, sc_ref[1:2, :], sc_ref[0:1, :])
        sh = jnp.where(is_ctx, sh_ref[1:2, :], sh_ref[0:1, :])
        emit(r0, y * (1.0 + sc) + sh)
        return carry

    lax.fori_loop(0, tm // ROW_CHUNK, scales, 0)
    lax.fori_loop(0, tm // ROW_CHUNK, body, 0)


def _rope(a, cos, s1, s2):
    return a * cos + pltpu.roll(a, HEAD_DIM - ROPE_FREQS, 1) * s1 + pltpu.roll(a, ROPE_FREQS, 1) * s2


def _norm_mm_kernel(*refs, n_ctx, tm, tn, mode, n_rope_tiles, n_q_tiles):
    if mode in ("rope", "qknorm_rope"):
        if mode == "qknorm_rope":
            x_ref, gain_ref, sh_ref, sc_ref, w_ref, cos_ref, s1_ref, s2_ref, hg_ref, o_ref, h_scr, rs_scr = refs
        else:
            x_ref, gain_ref, sh_ref, sc_ref, w_ref, cos_ref, s1_ref, s2_ref, o_ref, h_scr, rs_scr = refs
    else:
        x_ref, gain_ref, sh_ref, sc_ref, w_ref, o_ref, h_scr, rs_scr = refs
    i = pl.program_id(0)
    j = pl.program_id(1)

    @pl.when(j == 0)
    def _():
        def emit(r0, h):
            h_scr[pl.ds(r0, ROW_CHUNK), :] = h.astype(BF16)

        _modulated_rows(x_ref, gain_ref, sh_ref, sc_ref, rs_scr, i * tm, n_ctx, emit)

    acc = jnp.dot(h_scr[...], w_ref[...], preferred_element_type=F32)

    if mode == "none":
        o_ref[...] = acc.astype(o_ref.dtype)
    elif mode == "gelu":
        o_ref[...] = jax.nn.gelu(acc).astype(o_ref.dtype)
    else:
        @pl.when(j < n_rope_tiles)
        def _():
            for s in range(tn // HEAD_DIM):
                a = acc[:, s * HEAD_DIM:(s + 1) * HEAD_DIM]
                if mode == "qknorm_rope":
                    ms = jnp.mean(a * a, axis=-1, keepdims=True)
                    a = (a * lax.rsqrt(ms + EPS)) * hg_ref[:, s * HEAD_DIM:(s + 1) * HEAD_DIM]
                a = _rope(a, cos_ref[...], s1_ref[...], s2_ref[...])
                a = a * jnp.where(j < n_q_tiles, QK_FOLD, 1.0)
                o_ref[:, s * HEAD_DIM:(s + 1) * HEAD_DIM] = a.astype(o_ref.dtype)

        @pl.when(j >= n_rope_tiles)
        def _():
            o_ref[...] = acc.astype(o_ref.dtype)


def _norm_mm(xs, gain, sh, sc, w, *, n_ctx, mode, out_dtype, rope=None, head_gain=None, n_rope_cols=0,
             n_q_cols=0):
    nt, d = xs.shape
    n = w.shape[1]
    tm = _pick(nt, (768, 512, 256))
    tn = _pick(math.gcd(math.gcd(n, n_rope_cols), n_q_cols), (512, 256, 128))
    in_specs = [pl.BlockSpec((tm, d), lambda i, j: (i, 0)),
                pl.BlockSpec((1, d), lambda i, j: (0, 0)),
                pl.BlockSpec((2, d), lambda i, j: (0, 0)),
                pl.BlockSpec((2, d), lambda i, j: (0, 0)),
                pl.BlockSpec((d, tn), lambda i, j: (0, j))]
    args = [xs, gain, sh, sc, w]
    if mode in ("rope", "qknorm_rope"):
        assert n_rope_cols % tn == 0
        for t in rope:
            in_specs.append(pl.BlockSpec((tm, HEAD_DIM), lambda i, j: (i, 0)))
            args.append(t)
        if mode == "qknorm_rope":
            in_specs.append(pl.BlockSpec((1, tn), lambda i, j: (0, j)))
            args.append(head_gain)
    return pl.pallas_call(
        functools.partial(_norm_mm_kernel, n_ctx=n_ctx, tm=tm, tn=tn, mode=mode,
                          n_rope_tiles=n_rope_cols // tn, n_q_tiles=n_q_cols // tn),
        out_shape=jax.ShapeDtypeStruct((nt, n), out_dtype),
        grid=(nt // tm, n // tn),
        in_specs=in_specs,
        out_specs=pl.BlockSpec((tm, tn), lambda i, j: (i, j)),
        scratch_shapes=[pltpu.VMEM((tm, d), BF16), pltpu.VMEM((tm, 1), F32)],
        compiler_params=_params(("arbitrary", "arbitrary")),
        name="norm_mm_" + mode,
    )(*args)


def _resid_mm_kernel(a_ref, w_ref, r_ref, g_ref, o_ref, *, n_ctx, tm):
    i = pl.program_id(0)
    acc = jnp.dot(a_ref[...], w_ref[...], preferred_element_type=F32)
    rows = i * tm + lax.broadcasted_iota(jnp.int32, (tm, 1), 0)
    g = jnp.where(rows < n_ctx, g_ref[1:2, :], g_ref[0:1, :])
    o_ref[...] = r_ref[...] + g * acc


def _resid_mm(a, w, resid, gate, *, n_ctx):
    nt, k = a.shape
    n = w.shape[1]
    tm = _pick(nt, (1056, 768, 640, 512, 256))
    tn = _pick(n, (512, 256, 128))
    return pl.pallas_call(
        functools.partial(_resid_mm_kernel, n_ctx=n_ctx, tm=tm),
        out_shape=jax.ShapeDtypeStruct((nt, n), F32),
        grid=(nt // tm, n // tn),
        in_specs=[pl.BlockSpec((tm, k), lambda i, j: (i, 0)),
                  pl.BlockSpec((k, tn), lambda i, j: (0, j)),
                  pl.BlockSpec((tm, tn), lambda i, j: (i, j)),
                  pl.BlockSpec((2, tn), lambda i, j: (0, j))],
        out_specs=pl.BlockSpec((tm, tn), lambda i, j: (i, j)),
        compiler_params=_params(("arbitrary", "arbitrary")),
        name="resid_mm",
    )(a, w, resid, gate)


QK_FOLD = HEAD_DIM ** -0.5 * math.log2(math.e)
_NT_DIMS = (((1,), (1,)), ((), ()))


def _stack_heads(q_ref, r0, tq):
    return jnp.concatenate([q_ref[r0:r0 + tq, g * HEAD_DIM:(g + 1) * HEAD_DIM] for g in range(GROUP)], axis=0)


def _with_ones(v):
    return jnp.concatenate([v, jnp.ones(v.shape, v.dtype)], axis=1)


def _attn_win_kernel(sink_ref, q_ref, k_ref, v_ref, o_ref, s_buf, *, tq, nq, n_ctx, wk):
    h = pl.program_id(0)
    step = pl.program_id(1)
    nt = k_ref.shape[0]
    n_ctx_tiles = n_ctx // tq

    def window(t):
        tile = step * nq + t
        start = pl.multiple_of(jnp.clip(tile * tq - WINDOW, 0, nt - wk), WINDOW)
        return tile, start

    def s_tile(t):
        _, start = window(t)
        k_all = jnp.concatenate([k_ref[0:n_ctx, :], k_ref[pl.ds(start, wk), :]], axis=0)
        s_buf[t % 2] = lax.dot_general(_stack_heads(q_ref, t * tq, tq), k_all, _NT_DIMS,
                                       preferred_element_type=F32)

    def consume(t):
        tile, start = window(t)
        v_aug = _with_ones(jnp.concatenate([v_ref[0:n_ctx, :], v_ref[pl.ds(start, wk), :]], axis=0))
        c = lax.broadcasted_iota(jnp.int32, (tq, wk), 1)
        r = lax.broadcasted_iota(jnp.int32, (tq, wk), 0)
        rel = (c - r) + ((start - n_ctx) - (tile - n_ctx_tiles) * tq)
        reach = jnp.where(tile >= n_ctx_tiles, WINDOW, -1)
        bias = jnp.where(jnp.abs(rel) <= reach, jnp.where(c >= n_ctx - start, 0.0, NEG), NEG)
        bias = jnp.concatenate([jnp.zeros((tq, n_ctx), F32), bias], axis=1)
        for g in range(GROUP):
            s = s_buf[t % 2, g * tq:(g + 1) * tq, :] + bias
            sk = sink_ref[h * GROUP + g] * math.log2(math.e)
            m = jnp.maximum(jnp.max(s, axis=1, keepdims=True), sk)
            p = jnp.exp2(s - m)
            acc = jnp.dot(p.astype(BF16), v_aug, preferred_element_type=F32)
            o = acc[:, :HEAD_DIM] / (acc[:, HEAD_DIM:] + jnp.exp2(sk - m))
            o_ref[t * tq:(t + 1) * tq, g * HEAD_DIM:(g + 1) * HEAD_DIM] = o.astype(o_ref.dtype)

    s_tile(0)
    for t in range(nq):
        if t + 1 < nq:
            s_tile(t + 1)
        consume(t)


def _attn_win(qkv, sink, *, n_ctx, n_heads, n_kv):
    nt = qkv.shape[0]
    tq = n_ctx
    wk = tq + 2 * WINDOW
    assert tq % WINDOW == 0 and nt % tq == 0 and nt >= wk
    nq = _pick(nt // tq, (3, 2, 1))
    kcol, vcol = n_heads, n_heads + n_kv
    return pl.pallas_call(
        functools.partial(_attn_win_kernel, tq=tq, nq=nq, n_ctx=n_ctx, wk=wk),
        out_shape=jax.ShapeDtypeStruct((nt, n_heads * HEAD_DIM), BF16),
        grid=(n_kv, nt // (nq * tq)),
        in_specs=[pl.BlockSpec(memory_space=pltpu.SMEM),
                  pl.BlockSpec((nq * tq, GROUP * HEAD_DIM), lambda h, i: (i, h)),
                  pl.BlockSpec((nt, HEAD_DIM), lambda h, i: (0, kcol + h)),
                  pl.BlockSpec((nt, HEAD_DIM), lambda h, i: (0, vcol + h))],
        out_specs=pl.BlockSpec((nq * tq, GROUP * HEAD_DIM), lambda h, i: (i, h)),
        scratch_shapes=[pltpu.VMEM((2, GROUP * tq, n_ctx + wk), F32)],
        compiler_params=_params(("arbitrary", "arbitrary")),
        name="attn_win",
    )(sink, qkv, qkv, qkv)


def _attn_full_kernel(q_ref, k_ref, v_ref, o_ref, s_buf, m_scr, acc_scr, *, tq, tk, n_ctx, nh):
    i = pl.program_id(1)
    nt = k_ref.shape[0]
    n_chunks = nt // tk
    mq = GROUP * tq
    head_cols = lambda hh: slice(hh * HEAD_DIM, (hh + 1) * HEAD_DIM)

    def q_rows(hh):
        return jnp.concatenate([q_ref[:, head_cols(hh * GROUP + g)] for g in range(GROUP)], axis=0)

    def s_tile(c, slot):
        off = pl.multiple_of(c * tk, tk)
        for hh in range(nh):
            s_buf[slot, hh * mq:(hh + 1) * mq, :] = lax.dot_general(
                q_rows(hh), k_ref[pl.ds(off, tk), head_cols(hh)], _NT_DIMS, preferred_element_type=F32)

    def consume(c, slot):
        off = pl.multiple_of(c * tk, tk)
        for hh in range(nh):
            v_aug = _with_ones(v_ref[pl.ds(off, tk), head_cols(hh)])
            for g in range(GROUP):
                rows = slice(hh * mq + g * tq, hh * mq + (g + 1) * tq)
                s = s_buf[slot, rows, :]
                m_prev = m_scr[rows, :]
                m_new = jnp.maximum(m_prev, jnp.max(s, axis=1, keepdims=True))
                p = jnp.exp2(s - m_new)
                acc_scr[rows, :] = (jnp.exp2(m_prev - m_new) * acc_scr[rows, :]
                                    + jnp.dot(p.astype(BF16), v_aug, preferred_element_type=F32))
                m_scr[rows, :] = m_new

    def finish():
        acc = acc_scr[...]
        o = acc[:, :HEAD_DIM] / acc[:, HEAD_DIM:]
        for hg in range(nh * GROUP):
            o_ref[:, head_cols(hg)] = o[hg * tq:(hg + 1) * tq].astype(o_ref.dtype)

    @pl.when(i < n_ctx // tq)
    def _():
        for hh in range(nh):
            s = lax.dot_general(q_rows(hh), k_ref[0:n_ctx, head_cols(hh)], _NT_DIMS, preferred_element_type=F32)
            p = jnp.exp2(s - jnp.max(s, axis=1, keepdims=True))
            acc_scr[hh * mq:(hh + 1) * mq, :] = jnp.dot(p.astype(BF16), _with_ones(v_ref[0:n_ctx, head_cols(hh)]),
                                                        preferred_element_type=F32)
        finish()

    @pl.when(i >= n_ctx // tq)
    def _():
        m_scr[...] = jnp.full(m_scr.shape, NEG, F32)
        acc_scr[...] = jnp.zeros(acc_scr.shape, F32)
        s_tile(0, 0)

        def pair(j, carry):
            s_tile(2 * j + 1, 1)
            consume(2 * j, 0)
            s_tile(2 * j + 2, 0)
            consume(2 * j + 1, 1)
            return carry

        lax.fori_loop(0, (n_chunks - 1) // 2, pair, 0)
        if n_chunks % 2 == 1:
            consume(n_chunks - 1, 0)
        else:
            s_tile(n_chunks - 1, 1)
            consume(n_chunks - 2, 0)
            consume(n_chunks - 1, 1)
        finish()


def _attn_full(qkv, *, n_ctx, n_heads, n_kv):
    nt = qkv.shape[0]
    tq = _pick(n_ctx, (256, 128))
    tk = _pick(nt, (768, 512, 384, 256, 128))
    assert nt % tq == 0 and n_ctx % tq == 0
    nh = 1
    kblk, vblk = n_heads // nh, (n_heads + n_kv) // nh
    return pl.pallas_call(
        functools.partial(_attn_full_kernel, tq=tq, tk=tk, n_ctx=n_ctx, nh=nh),
        out_shape=jax.ShapeDtypeStruct((nt, n_heads * HEAD_DIM), BF16),
        grid=(n_kv // nh, nt // tq),
        in_specs=[pl.BlockSpec((tq, nh * GROUP * HEAD_DIM), lambda h, i: (i, h)),
                  pl.BlockSpec((nt, nh * HEAD_DIM), lambda h, i: (0, kblk + h)),
                  pl.BlockSpec((nt, nh * HEAD_DIM), lambda h, i: (0, vblk + h))],
        out_specs=pl.BlockSpec((tq, nh * GROUP * HEAD_DIM), lambda h, i: (i, h)),
        scratch_shapes=[pltpu.VMEM((2, nh * GROUP * tq, tk), F32), pltpu.VMEM((nh * GROUP * tq, 1), F32),
                        pltpu.VMEM((nh * GROUP * tq, 2 * HEAD_DIM), F32)],
        compiler_params=_params(("arbitrary", "arbitrary")),
        name="attn_full",
    )(qkv, qkv, qkv)


def _rglru_kernel(u_ref, gate_ref, cw_ref, cb_ref, wr_ref, wi_ref, br_ref, bi_ref, lam_ref, o_ref, rec_scr,
                  ga_scr, gb_scr, *, tc, n_chunks, n_ctx_chunks):
    nt, ch = u_ref.shape
    row = lax.broadcasted_iota(jnp.int32, (tc, 1), 0)

    def conv_chunk(c):
        t0 = pl.multiple_of(c * tc, tc)
        x0 = u_ref[pl.ds(t0, tc), :]
        seg_start = (c == 0) | (c == n_ctx_chunks)
        seg_end = (c == n_ctx_chunks - 1) | (c == n_chunks - 1)
        p0 = pl.multiple_of(jnp.maximum(t0 - 8, 0), 8)
        n0 = pl.multiple_of(jnp.minimum(t0 + tc, nt - 8), 8)
        prev = u_ref[pl.ds(p0, 8), :] * jnp.where(seg_start, 0.0, 1.0)
        nxt = u_ref[pl.ds(n0, 8), :] * jnp.where(seg_end, 0.0, 1.0)
        xm1 = jnp.where(row == 0, prev[7:8, :], pltpu.roll(x0, 1, 0))
        xm2 = jnp.where(row == 0, prev[6:7, :], jnp.where(row == 1, prev[7:8, :], pltpu.roll(x0, 2, 0)))
        xp1 = jnp.where(row == tc - 1, nxt[0:1, :], pltpu.roll(x0, tc - 1, 0))
        uc = cw_ref[0:1, :] * xm2 + cw_ref[1:2, :] * xm1 + cw_ref[2:3, :] * x0 + cw_ref[3:4, :] * xp1
        return t0, uc + cb_ref[...]

    def coeffs(uc, d):
        ub = uc.astype(BF16)
        r = jax.nn.sigmoid(jnp.dot(ub, wr_ref[d, 0], preferred_element_type=F32) + br_ref[d:d + 1, :])
        gi = jax.nn.sigmoid(jnp.dot(ub, wi_ref[d, 0], preferred_element_type=F32) + bi_ref[d:d + 1, :])
        x = -lam_ref[d:d + 1, :]
        softplus = jnp.maximum(x, 0.0) + jnp.log1p(jnp.exp(-jnp.abs(x)))
        log_a = (-LRU_C * r) * softplus
        a = jnp.exp(log_a)
        one_minus_a2 = -jnp.tanh(log_a) * (a * a + 1.0)
        return a, jnp.sqrt(one_minus_a2) * (gi * uc)

    def scan_steps(a, b, idx, n, first_step, reverse, axis=0):
        s = first_step
        while s < n:
            if reverse:
                keep = idx < n - s
                a_sh, b_sh = pltpu.roll(a, a.shape[axis] - s, axis), pltpu.roll(b, b.shape[axis] - s, axis)
            else:
                keep = idx >= s
                a_sh, b_sh = pltpu.roll(a, s, axis), pltpu.roll(b, s, axis)
            b = a * jnp.where(keep, b_sh, 0.0) + b
            a = a * jnp.where(keep, a_sh, 1.0)
            s *= 2
        return a, b

    n_groups = tc // SUBLANES
    grow = lax.broadcasted_iota(jnp.int32, (n_groups, 1), 0)

    def scan_chunk(a, b, h0, reverse):
        sub = lax.broadcasted_iota(jnp.int32, (n_groups, SUBLANES, 1), 1)
        a, b = scan_steps(a.reshape(n_groups, SUBLANES, ch), b.reshape(n_groups, SUBLANES, ch), sub, SUBLANES, 1,
                          reverse, axis=1)
        a, b = a.reshape(tc, ch), b.reshape(tc, ch)
        end_row = 0 if reverse else SUBLANES - 1

        def group_ends(scr, v):
            for k in range(ch // LANES):
                scr[k] = v[:, k * LANES:(k + 1) * LANES]
            return jnp.concatenate([scr[k, pl.ds(end_row, n_groups, stride=SUBLANES), :]
                                    for k in range(ch // LANES)], axis=1)

        a_end = group_ends(ga_scr, a)
        b_end = group_ends(gb_scr, b)
        entry = n_groups - 1 if reverse else 0
        b_end = jnp.where(grow == entry, b_end + a_end * h0, b_end)
        _, h_end = scan_steps(a_end, b_end, grow, n_groups, 1, reverse)
        h_in = jnp.where(grow == entry, h0, pltpu.roll(h_end, n_groups - 1 if reverse else 1, 0))
        h_in = jnp.broadcast_to(h_in[:, None, :], (n_groups, SUBLANES, h_in.shape[1])).reshape(tc, h_in.shape[1])
        return a * h_in + b

    def fwd(c, h):
        t0, uc = conv_chunk(c)
        a, b = coeffs(uc, 0)
        hh = scan_chunk(a, b, h, False)
        rec_scr[pl.ds(t0, tc), :] = hh
        return hh[tc - 1:tc, :]

    def bwd(j, h):
        c = jnp.where(j < n_ctx_chunks, n_ctx_chunks - 1 - j, n_chunks - 1 - (j - n_ctx_chunks))
        t0, uc = conv_chunk(c)
        a, b = coeffs(uc, 1)
        hh = scan_chunk(a, b, h, True)
        rec = rec_scr[pl.ds(t0, tc), :] + hh
        o_ref[pl.ds(t0, tc), :] = (gate_ref[pl.ds(t0, tc), :].astype(F32) * rec).astype(o_ref.dtype)
        return hh[0:1, :]

    zero = jnp.zeros((1, ch), F32)
    lax.fori_loop(0, n_chunks, fwd, zero)
    lax.fori_loop(0, n_chunks, bwd, zero)


def _rglru(u, gate, conv_w, conv_b, w_ra, w_ix, b_ra, b_ix, lam, *, n_ctx):
    nt, d = u.shape
    nb, ch = w_ra.shape[1], w_ra.shape[2]
    tc = _pick(n_ctx, (256, 128))
    assert n_ctx % tc == 0 and nt % tc == 0
    col = lambda h: (0, h)
    return pl.pallas_call(
        functools.partial(_rglru_kernel, tc=tc, n_chunks=nt // tc, n_ctx_chunks=n_ctx // tc),
        out_shape=jax.ShapeDtypeStruct((nt, d), BF16),
        grid=(nb,),
        in_specs=[pl.BlockSpec((nt, ch), col),
                  pl.BlockSpec((nt, ch), col),
                  pl.BlockSpec((conv_w.shape[0], ch), col),
                  pl.BlockSpec((1, ch), col),
                  pl.BlockSpec((2, 1, ch, ch), lambda h: (0, h, 0, 0)),
                  pl.BlockSpec((2, 1, ch, ch), lambda h: (0, h, 0, 0)),
                  pl.BlockSpec((2, ch), col),
                  pl.BlockSpec((2, ch), col),
                  pl.BlockSpec((2, ch), col)],
        out_specs=pl.BlockSpec((nt, ch), col),
        scratch_shapes=[pltpu.VMEM((nt, ch), F32), pltpu.VMEM((ch // LANES, tc, LANES), F32),
                        pltpu.VMEM((ch // LANES, tc, LANES), F32)],
        compiler_params=_params(("arbitrary",)),
        name="rglru",
    )(u, gate, conv_w, conv_b, w_ra, w_ix, b_ra, b_ix, lam)


def _ffn_pre_kernel(x_ref, gain_ref, sh_ref, sc_ref, wr_ref, hp_ref, aff_ref, h_scr, rs_scr, *, n_ctx, tm, n_experts):
    i = pl.program_id(0)
    half = x_ref.shape[1] // 2

    def emit(r0, h):
        hb = h.astype(BF16)
        h_scr[pl.ds(r0, ROW_CHUNK), :] = hb
        bits = pltpu.bitcast(hb.astype(F32), jnp.uint32)
        hp_ref[pl.ds(r0, ROW_CHUNK), :] = (bits[:, :half] & jnp.uint32(0xFFFF0000)) | (bits[:, half:] >> 16)

    _modulated_rows(x_ref, gain_ref, sh_ref, sc_ref, rs_scr, i * tm, n_ctx, emit)
    logits = jnp.dot(h_scr[...], wr_ref[...], preferred_element_type=F32)
    lane = lax.broadcasted_iota(jnp.int32, logits.shape, 1)
    logits = jnp.where(lane < n_experts, logits, NEG)
    e = jnp.exp(logits - jnp.max(logits, axis=-1, keepdims=True))
    aff_ref[...] = e / jnp.sum(e, axis=-1, keepdims=True)


def _ffn_pre(xs, gain, sh, sc, w_router_pad, *, n_ctx, n_experts):
    nt, d = xs.shape
    tm = _pick(nt, (768, 512, 256))
    return pl.pallas_call(
        functools.partial(_ffn_pre_kernel, n_ctx=n_ctx, tm=tm, n_experts=n_experts),
        out_shape=(jax.ShapeDtypeStruct((nt, d // 2), jnp.uint32),
                   jax.ShapeDtypeStruct((nt, LANES), F32)),
        grid=(nt // tm,),
        in_specs=[pl.BlockSpec((tm, d), lambda i: (i, 0)),
                  pl.BlockSpec((1, d), lambda i: (0, 0)),
                  pl.BlockSpec((2, d), lambda i: (0, 0)),
                  pl.BlockSpec((2, d), lambda i: (0, 0)),
                  pl.BlockSpec((d, LANES), lambda i: (0, 0))],
        out_specs=(pl.BlockSpec((tm, d // 2), lambda i: (i, 0)),
                   pl.BlockSpec((tm, LANES), lambda i: (i, 0))),
        scratch_shapes=[pltpu.VMEM((tm, d), BF16), pltpu.VMEM((tm, 1), F32)],
        compiler_params=_params(("arbitrary",)),
        name="ffn_pre",
    )(xs, gain, sh, sc, w_router_pad)


def _pack_bf16_pair(hi, lo):
    hi_bits = pltpu.bitcast(hi.astype(BF16).astype(F32), jnp.uint32)
    lo_bits = pltpu.bitcast(lo.astype(BF16).astype(F32), jnp.uint32)
    return (hi_bits & jnp.uint32(0xFFFF0000)) | (lo_bits >> 16)


def _unpack_bf16_pair(w):
    return (pltpu.bitcast(w & jnp.uint32(0xFFFF0000), F32).astype(BF16),
            pltpu.bitcast(w << 16, F32).astype(BF16))


def _moe_kernel(idx_ref, hp_hbm, g_ref, wg_ref, wu_ref, wda_ref, wdb_ref, o_ref, land, xb, act, sem,
                *, nf, tf, per):
    e = pl.program_id(0)
    s = pl.program_id(1)
    n_experts = pl.num_programs(0)
    n_steps = pl.num_programs(1)
    rows, half = xb.shape[0], land.shape[1]

    def gather_row(expert, p):
        src = idx_ref[expert, p]
        pltpu.make_async_copy(hp_hbm.at[pl.ds(src, 1), :], land.at[pl.ds(p, 1), :], sem).start()

    def wait_rows():
        pltpu.make_async_copy(hp_hbm.at[pl.ds(0, land.shape[0]), :], land, sem).wait()

    def gather_ahead():
        nxt = lax.rem(e + 1, n_experts)
        for k in range(per):
            gather_row(nxt, s * per + k)

    @pl.when((e == 0) & (s == 0))
    def _():
        def body(p, carry):
            gather_row(0, p)
            return carry

        lax.fori_loop(0, land.shape[0], body, 0)

    @pl.when(s == 0)
    def _():
        wait_rows()
        xb[:, :half], xb[:, half:] = _unpack_bf16_pair(land[0:rows, :])

    @pl.when(s < nf)
    def _():
        gather_ahead()
        x = xb[...]
        a = jnp.dot(x, wg_ref[0, 0].astype(BF16), preferred_element_type=F32)
        u = jnp.dot(x, wu_ref[0, 0].astype(BF16), preferred_element_type=F32)
        act[s] = (a * jax.nn.sigmoid(a) * u).astype(BF16)

    @pl.when(s >= nf)
    def _():
        gather_ahead()

        def down(wd_ref):
            acc = jnp.dot(act[0], wd_ref[0, 0, 0:tf, :].astype(BF16), preferred_element_type=F32)
            for f in range(1, nf):
                acc = acc + jnp.dot(act[f], wd_ref[0, 0, f * tf:(f + 1) * tf, :].astype(BF16),
                                    preferred_element_type=F32)
            return acc * g_ref[0]

        o_ref[0] = _pack_bf16_pair(down(wda_ref), down(wdb_ref))

    @pl.when((e == n_experts - 1) & (s == n_steps - 1))
    def _():
        wait_rows()


def _moe(idx, hp, gates, w_gate, w_up, w_down, layer):
    n_experts, rows = idx.shape
    half = hp.shape[1]
    dexp = w_gate.shape[3]
    tf = _pick(dexp, (256, 128))
    tp = _pick(half, (512, 256, 128))
    nf, nd = dexp // tf, half // tp
    per = -(-rows // ((nf + nd) * SUBLANES)) * SUBLANES
    rows_pad = per * (nf + nd)
    idx = jnp.concatenate([idx, jnp.broadcast_to(idx[:, -1:], (n_experts, rows_pad - rows))], axis=1)
    up_map = lambda e, s, idx: (layer, e, 0, jnp.minimum(s, nf - 1))
    grid_spec = pltpu.PrefetchScalarGridSpec(
        num_scalar_prefetch=1,
        grid=(n_experts, nf + nd),
        in_specs=[pl.BlockSpec(memory_space=pl.ANY),
                  pl.BlockSpec((1, rows, 1), lambda e, s, idx: (e, 0, 0)),
                  pl.BlockSpec((1, 1, 2 * half, tf), up_map),
                  pl.BlockSpec((1, 1, 2 * half, tf), up_map),
                  pl.BlockSpec((1, 1, dexp, tp), lambda e, s, idx: (layer, e, 0, jnp.maximum(s - nf, 0))),
                  pl.BlockSpec((1, 1, dexp, tp), lambda e, s, idx: (layer, e, 0, jnp.maximum(s - nf, 0) + nd))],
        out_specs=pl.BlockSpec((1, rows, tp), lambda e, s, idx: (e, 0, jnp.maximum(s - nf, 0))),
        scratch_shapes=[pltpu.VMEM((rows_pad, half), jnp.uint32),
                        pltpu.VMEM((rows, 2 * half), BF16),
                        pltpu.VMEM((nf, rows, tf), BF16),
                        pltpu.SemaphoreType.DMA(())],
    )
    return pl.pallas_call(
        functools.partial(_moe_kernel, nf=nf, tf=tf, per=per),
        out_shape=jax.ShapeDtypeStruct((n_experts, rows, half), jnp.uint32),
        grid_spec=grid_spec,
        compiler_params=_params(("arbitrary", "arbitrary")),
        name="moe_experts",
    )(idx, hp, gates, w_gate, w_up, w_down, w_down)


COMBINE_TB = 256
COMBINE_W = 64


def _combine_kernel(src_ref, nwin_ref, y_hbm, slots_ref, pos_ref, rep_ref, x_ref, g_ref, *rest,
                    n_experts, rows, n_ctx, final):
    if final:
        final_gain_ref, o_ref, ybuf, yextra, sem, sem_extra = rest
    else:
        final_gain_ref = None
        o_ref, ybuf, yextra, sem, sem_extra = rest
    b = pl.program_id(0)
    n_blocks = pl.num_programs(0)
    tb, d = x_ref.shape
    half = d // 2
    n_windows = slots_ref.shape[1]
    w = COMBINE_W
    slot = lax.rem(b, 2)

    q = pos_ref[...] + 1
    q_hi = (q >> 5).astype(F32).astype(BF16)
    q_lo = (q & 31).astype(F32).astype(BF16)
    pos_wide = (32.0 * jnp.dot(q_hi, rep_ref[...], preferred_element_type=F32)
                + jnp.dot(q_lo, rep_ref[...], preferred_element_type=F32) - 1.0)

    def fetch(block, window, dst, dma_sem):
        for e in range(n_experts):
            src = pl.multiple_of(e * rows + src_ref[(block * n_windows + window) * n_experts + e], SUBLANES)
            pltpu.make_async_copy(y_hbm.at[pl.ds(src, w), :], dst.at[pl.ds(e * w, w), :], dma_sem).start()

    def wait(dst, dma_sem):
        pltpu.make_async_copy(y_hbm.at[pl.ds(0, n_experts * w), :], dst, dma_sem).wait()

    def expand(buf, window, first):
        window_slots = slots_ref[0, pl.ds(window, 1), :].astype(F32)
        onehot = jnp.where(pos_wide == window_slots, 1.0, 0.0).astype(BF16)
        chunk = _pick(half, (512, 256, 128))
        for k in range(half // chunk):
            cols = slice(k * chunk, (k + 1) * chunk)
            cols_hi = slice(half + k * chunk, half + (k + 1) * chunk)
            hi, lo = _unpack_bf16_pair(buf[:, cols])
            a_hi = jnp.dot(onehot, hi, preferred_element_type=F32)
            a_lo = jnp.dot(onehot, lo, preferred_element_type=F32)
            if first:
                o_ref[:, cols] = a_hi
                o_ref[:, cols_hi] = a_lo
            else:
                o_ref[:, cols] += a_hi
                o_ref[:, cols_hi] += a_lo

    @pl.when(b == 0)
    def _():
        fetch(0, 0, ybuf.at[0], sem.at[0])

    wait(ybuf.at[slot], sem.at[slot])

    @pl.when(b + 1 < n_blocks)
    def _():
        fetch(b + 1, 0, ybuf.at[1 - slot], sem.at[1 - slot])

    expand(ybuf.at[slot], 0, True)

    def extra(window, carry):
        fetch(b, window, yextra, sem_extra)
        wait(yextra, sem_extra)
        expand(yextra, window, False)
        return carry

    lax.fori_loop(1, nwin_ref[b], extra, 0)

    row = b * tb + lax.broadcasted_iota(jnp.int32, (tb, 1), 0)
    g = jnp.where(row < n_ctx, g_ref[1:2, :], g_ref[0:1, :])
    x_new = x_ref[...] + g * o_ref[...]
    if final_gain_ref is None:
        o_ref[...] = x_new
    else:
        ms = jnp.mean(x_new * x_new, axis=-1, keepdims=True)
        o_ref[...] = (x_new * lax.rsqrt(ms + EPS)) * final_gain_ref[...]


def _combine_plan(pos, r):
    nt, n_experts = pos.shape
    tb, w = COMBINE_TB, COMBINE_W
    served = w - SUBLANES
    nb, nw = nt // tb, -(-tb // served)
    per_block = jnp.sum((pos >= 0).astype(jnp.int32).reshape(nb, tb, n_experts), axis=1)
    stop = jnp.cumsum(per_block, axis=0)
    start = stop - per_block
    pos = jnp.pad(pos, ((0, 0), (0, LANES - n_experts)), constant_values=-1)
    first = start[:, None, :] + jnp.arange(nw, dtype=jnp.int32)[None, :, None] * served
    src = jnp.minimum(first // SUBLANES * SUBLANES, r - w)
    slot = src[..., None] + jnp.arange(w, dtype=jnp.int32)
    last = jnp.minimum(first + served, stop[:, None, :])
    valid = (slot >= first[..., None]) & (slot < last[..., None])
    slots = jnp.where(valid, slot, -2).reshape(nb, nw, n_experts * w)
    n_win = jnp.maximum(1, jnp.max((stop - start + served - 1) // served, axis=1)).astype(jnp.int32)
    lane_expert = jnp.arange(n_experts * w, dtype=jnp.int32) // w
    rep = (jnp.arange(LANES, dtype=jnp.int32)[:, None] == lane_expert[None, :]).astype(BF16)
    return src.reshape(-1), n_win, slots, pos, rep


def _combine(y_packed, pos, xs, gate, *, n_ctx, final_gain=None):
    nt, d = xs.shape
    n_experts, r, half = y_packed.shape
    tb = COMBINE_TB
    assert nt % tb == 0 and r >= COMBINE_W and r % SUBLANES == 0 and half % LANES == 0
    src, n_win, slots, pos, rep = _combine_plan(pos, r)
    nw = slots.shape[1]
    final = final_gain is not None
    in_specs = [pl.BlockSpec(memory_space=pl.ANY),
                pl.BlockSpec((1, nw, n_experts * COMBINE_W), lambda b, src, nwin: (b, 0, 0)),
                pl.BlockSpec((tb, LANES), lambda b, src, nwin: (b, 0)),
                pl.BlockSpec((LANES, n_experts * COMBINE_W), lambda b, src, nwin: (0, 0)),
                pl.BlockSpec((tb, d), lambda b, src, nwin: (b, 0)),
                pl.BlockSpec((2, d), lambda b, src, nwin: (0, 0))]
    args = [src, n_win, y_packed.reshape(n_experts * r, half), slots, pos, rep, xs, gate]
    if final:
        assert n_ctx % tb == 0
        in_specs.append(pl.BlockSpec((1, d), lambda b, src, nwin: (0, 0)))
        args.append(final_gain)
        out_rows = nt - n_ctx
        out_map = lambda b, src, nwin: (jnp.maximum(b - n_ctx // tb, 0), 0)
    else:
        out_rows = nt
        out_map = lambda b, src, nwin: (b, 0)
    grid_spec = pltpu.PrefetchScalarGridSpec(
        num_scalar_prefetch=2,
        grid=(nt // tb,),
        in_specs=in_specs,
        out_specs=pl.BlockSpec((tb, d), out_map),
        scratch_shapes=[pltpu.VMEM((2, n_experts * COMBINE_W, half), jnp.uint32),
                        pltpu.VMEM((n_experts * COMBINE_W, half), jnp.uint32),
                        pltpu.SemaphoreType.DMA((2,)),
                        pltpu.SemaphoreType.DMA(())],
    )
    return pl.pallas_call(
        functools.partial(_combine_kernel, n_experts=n_experts, rows=r, n_ctx=n_ctx, final=final),
        out_shape=jax.ShapeDtypeStruct((out_rows, d), F32),
        grid_spec=grid_spec,
        compiler_params=_params(("arbitrary",)),
        name="moe_combine",
    )(*args)


def _rope_tables(seq, n_ctx):
    rows = seq // GRID_W
    row = jnp.repeat(jnp.arange(rows), GRID_W).astype(F32)
    col = jnp.tile(jnp.arange(GRID_W), rows).astype(F32)
    inv_freq = ROPE_THETA ** (-jnp.arange(ROPE_FREQS, dtype=F32) / ROPE_FREQS)
    ang_r = row[:, None] * inv_freq
    ang_c = col[:, None] * inv_freq
    zero = jnp.zeros_like(ang_r)
    cos = jnp.concatenate([jnp.cos(ang_r), jnp.cos(ang_r), jnp.cos(ang_c), jnp.cos(ang_c)], axis=1)
    s1 = jnp.concatenate([-jnp.sin(ang_r), zero, -jnp.sin(ang_c), zero], axis=1)
    s2 = jnp.concatenate([zero, jnp.sin(ang_r), zero, jnp.sin(ang_c)], axis=1)
    pad = lambda t, v: jnp.concatenate([jnp.full((n_ctx, HEAD_DIM), v, F32), t], axis=0)
    return pad(cos, 1.0), pad(s1, 0.0), pad(s2, 0.0)


def _route(aff, n_ctx, n_experts):
    nt = aff.shape[0]
    a_ctx = aff[:n_ctx, :n_experts]
    a_lat = aff[n_ctx:, :n_experts]
    cap_l = CAPACITY_FACTOR * (nt - n_ctx) // n_experts
    cap_c = CAPACITY_FACTOR * n_ctx // n_experts
    g_l, i_l = lax.top_k(a_lat.T, cap_l)
    g_c, i_c = lax.top_k(a_ctx.T, cap_c)
    idx = jnp.concatenate([i_l + n_ctx, i_c], axis=1).astype(jnp.int32)
    gates = jnp.concatenate([g_l, g_c], axis=1)
    idx, gates = lax.sort((idx, gates), dimension=1, num_keys=1)

    def slots(a, kth, cap, base):
        above, equal = a > kth[None, :], a == kth[None, :]
        room = cap - jnp.sum(above, axis=0, dtype=jnp.int32)
        picked = above | (equal & (jnp.cumsum(equal, axis=0, dtype=jnp.int32) <= room[None, :]))
        return jnp.where(picked, base + jnp.cumsum(picked, axis=0, dtype=jnp.int32) - 1, -1)

    pos = jnp.concatenate([slots(a_ctx, g_c[:, -1], cap_c, 0), slots(a_lat, g_l[:, -1], cap_l, cap_c)], axis=0)
    return idx, gates, pos


def kernel(x, c, ctx, c_ctx, mod_down, mod_up, mod_bias, norm_mix, norm_ffn, norm_final, a_w_qkv, a_w_o, a_sink, b_w_qkv, b_w_o, b_q_norm, b_k_norm, c_w_in, c_conv_w, c_conv_b, c_w_ra, c_b_ra, c_w_ix, c_b_ix, c_lambda, c_w_out, moe_router, moe_w_gate, moe_w_up, moe_w_down):
    assert x.shape[0] == 1 and ctx.shape[0] == 1
    seq, d = x.shape[1], x.shape[2]
    n_ctx = ctx.shape[1]
    depth = mod_down.shape[0]
    n_experts = moe_router.shape[2]
    n_heads = a_w_o.shape[1] // HEAD_DIM
    n_kv = n_heads // GROUP
    d_rnn = c_w_out.shape[1]

    xs = jnp.concatenate([ctx[0], x[0]], axis=0)
    cond = jnp.concatenate([c, c_ctx[None, :], jnp.zeros((6, d), F32)], axis=0)
    rope = _rope_tables(seq, n_ctx)
    zero_bias = jnp.zeros((1, mod_down.shape[2]), F32)

    for i in range(depth):
        low = _small_mm(cond, mod_down[i], zero_bias, silu=True)
        mod = _small_mm(low, mod_up[i], mod_bias[i][None, :], silu=False)[:2].reshape(2, 6, d)
        sh1, sc1, g1, sh2, sc2, g2 = (mod[:, k, :] for k in range(6))
        gain_mix = norm_mix[i][None, :]
        kind, j = i % 3, i // 3
        if kind == 0:
            qkv = _norm_mm(xs, gain_mix, sh1, sc1, a_w_qkv[j].astype(BF16), n_ctx=n_ctx, mode="rope",
                           out_dtype=BF16, rope=rope, n_rope_cols=(n_heads + n_kv) * HEAD_DIM,
                           n_q_cols=n_heads * HEAD_DIM)
            y = _attn_win(qkv, a_sink[j], n_ctx=n_ctx, n_heads=n_heads, n_kv=n_kv)
            w_out = a_w_o[j]
        elif kind == 1:
            head_gain = jnp.concatenate([jnp.tile(b_q_norm[j], n_heads), jnp.tile(b_k_norm[j], n_kv),
                                         jnp.ones((n_kv * HEAD_DIM,), F32)])[None, :]
            qkv = _norm_mm(xs, gain_mix, sh1, sc1, b_w_qkv[j].astype(BF16), n_ctx=n_ctx, mode="qknorm_rope",
                           out_dtype=BF16, rope=rope, head_gain=head_gain,
                           n_rope_cols=(n_heads + n_kv) * HEAD_DIM, n_q_cols=n_heads * HEAD_DIM)
            y = _attn_full(qkv, n_ctx=n_ctx, n_heads=n_heads, n_kv=n_kv)
            w_out = b_w_o[j]
        else:
            gate = _norm_mm(xs, gain_mix, sh1, sc1, c_w_in[j][:, :d_rnn].astype(BF16), n_ctx=n_ctx,
                            mode="gelu", out_dtype=BF16)
            u = _norm_mm(xs, gain_mix, sh1, sc1, c_w_in[j][:, d_rnn:].astype(BF16), n_ctx=n_ctx,
                         mode="none", out_dtype=F32)
            y = _rglru(u, gate, c_conv_w[j], c_conv_b[j][None, :], c_w_ra[j].astype(BF16),
                       c_w_ix[j].astype(BF16), c_b_ra[j], c_b_ix[j], c_lambda[j], n_ctx=n_ctx)
            w_out = c_w_out[j]
        xs = _resid_mm(y, w_out.astype(BF16), xs, g1, n_ctx=n_ctx)

        w_router = jnp.pad(moe_router[i], ((0, 0), (0, LANES - n_experts))).astype(BF16)
        hp, aff = _ffn_pre(xs, norm_ffn[i][None, :], sh2, sc2, w_router, n_ctx=n_ctx, n_experts=n_experts)
        idx, gates, pos = _route(aff, n_ctx, n_experts)
        yexp = _moe(idx, hp, gates[:, :, None], moe_w_gate, moe_w_up, moe_w_down, i)
        last = i == depth - 1
        xs = _combine(yexp, pos, xs, g2, n_ctx=n_ctx, final_gain=norm_final[None, :] if last else None)

    return xs[None]
```

```python
import functools
import math

import jax
import jax.numpy as jnp
from jax import lax
from jax.experimental import pallas as pl
from jax.experimental.pallas import tpu as pltpu

F32 = jnp.float32
BF16 = jnp.bfloat16

HEAD_DIM = 128
GROUP = 4
WINDOW = 128
GRID_W = 64
ROPE_THETA = 10000.0
ROPE_FREQS = HEAD_DIM // 4
CAPACITY_FACTOR = 2
LRU_C = 8.0
EPS = 1e-6
NEG = -1e30

V7X_VMEM_BYTES = 64 * 1024 * 1024
VMEM_LIMIT = V7X_VMEM_BYTES - 8 * 1024 * 1024
LANES = 128
SUBLANES = 8


def _pick(n, candidates):
    for c in candidates:
        if c <= n and n % c == 0:
            return c
    return n


def _params(sem):
    return pltpu.CompilerParams(dimension_semantics=sem, vmem_limit_bytes=VMEM_LIMIT)


def _small_mm_kernel(a_ref, w_ref, b_ref, o_ref, *, silu):
    a = a_ref[...]
    if silu:
        a = a * jax.nn.sigmoid(a)
    acc = jnp.dot(a.astype(BF16), w_ref[...].astype(BF16), preferred_element_type=F32)
    o_ref[...] = acc + b_ref[...]


def _small_mm(a, w, bias, *, silu):
    m, k = a.shape
    n = w.shape[1]
    tn = _pick(n, (2048, 1024, 512, 256, 128))
    return pl.pallas_call(
        functools.partial(_small_mm_kernel, silu=silu),
        out_shape=jax.ShapeDtypeStruct((m, n), F32),
        grid=(n // tn,),
        in_specs=[pl.BlockSpec((m, k), lambda j: (0, 0)),
                  pl.BlockSpec((k, tn), lambda j: (0, j)),
                  pl.BlockSpec((1, tn), lambda j: (0, j))],
        out_specs=pl.BlockSpec((m, tn), lambda j: (0, j)),
        compiler_params=_params(("arbitrary",)),
        name="small_mm",
    )(a, w, bias)


ROW_CHUNK = 64


def _modulated_rows(x_ref, gain_ref, sh_ref, sc_ref, rs_scr, row0, n_ctx, emit):
    tm = x_ref.shape[0]

    def scales(r, carry):
        r0 = pl.multiple_of(r * ROW_CHUNK, ROW_CHUNK)
        x = x_ref[pl.ds(r0, ROW_CHUNK), :]
        rs_scr[pl.ds(r0, ROW_CHUNK), :] = lax.rsqrt(jnp.mean(x * x, axis=-1, keepdims=True) + EPS)
        return carry

    def body(r, carry):
        r0 = pl.multiple_of(r * ROW_CHUNK, ROW_CHUNK)
        y = (x_ref[pl.ds(r0, ROW_CHUNK), :] * rs_scr[pl.ds(r0, ROW_CHUNK), :]) * gain_ref[...]
        is_ctx = row0 + r0 < n_ctx
        sc = jnp.where(is_ctx, sc_ref[1:2, :], sc_ref[0:1, :])
        sh = jnp.where(is_ctx, sh_ref[1:2, :], sh_ref[0:1, :])
        emit(r0, y * (1.0 + sc) + sh)
        return carry

    lax.fori_loop(0, tm // ROW_CHUNK, scales, 0)
    lax.fori_loop(0, tm // ROW_CHUNK, body, 0)


def _rope(a, cos, s1, s2):
    return a * cos + pltpu.roll(a, HEAD_DIM - ROPE_FREQS, 1) * s1 + pltpu.roll(a, ROPE_FREQS, 1) * s2


def _norm_mm_kernel(*refs, n_ctx, tm, tn, mode, n_rope_tiles, n_q_tiles, prenormed):
    refs = list(refs)
    if prenormed:
        h_scr = refs.pop(0)
    else:
        x_ref, gain_ref, sh_ref, sc_ref = refs[:4]
        h_scr, rs_scr = refs[-2:]
        refs = refs[4:-2]
    if mode == "qknorm_rope":
        w_ref, cos_ref, s1_ref, s2_ref, hg_ref, o_ref = refs
    elif mode == "rope":
        w_ref, cos_ref, s1_ref, s2_ref, o_ref = refs
    else:
        w_ref, o_ref = refs
    i = pl.program_id(0)
    j = pl.program_id(1)

    if not prenormed:
        @pl.when(j == 0)
        def _():
            def emit(r0, h):
                h_scr[pl.ds(r0, ROW_CHUNK), :] = h.astype(BF16)

            _modulated_rows(x_ref, gain_ref, sh_ref, sc_ref, rs_scr, i * tm, n_ctx, emit)

    def matmul(cols=slice(None)):
        return jnp.dot(h_scr[...], w_ref[:, cols], preferred_element_type=F32)

    if mode == "none":
        o_ref[...] = matmul().astype(o_ref.dtype)
    elif mode == "gelu":
        o_ref[...] = jax.nn.gelu(matmul()).astype(o_ref.dtype)
    else:
        shared = matmul() if mode == "qknorm_rope" else None
        mxu_cols = tn if mode == "qknorm_rope" else min(tn, 2 * HEAD_DIM)

        @pl.when(j < n_rope_tiles)
        def _():
            for c0 in range(0, tn, mxu_cols):
                acc = matmul(slice(c0, c0 + mxu_cols)) if shared is None else shared
                for s in range(mxu_cols // HEAD_DIM):
                    cols = slice(c0 + s * HEAD_DIM, c0 + (s + 1) * HEAD_DIM)
                    a = acc[:, s * HEAD_DIM:(s + 1) * HEAD_DIM]
                    if mode == "qknorm_rope":
                        ms = jnp.mean(a * a, axis=-1, keepdims=True)
                        a = (a * lax.rsqrt(ms + EPS)) * hg_ref[:, cols]
                    a = _rope(a, cos_ref[...], s1_ref[...], s2_ref[...])
                    a = a * jnp.where(j < n_q_tiles, QK_FOLD, 1.0)
                    o_ref[:, cols] = a.astype(o_ref.dtype)

        @pl.when(j >= n_rope_tiles)
        def _():
            o_ref[...] = (matmul() if shared is None else shared).astype(o_ref.dtype)


def _norm_mm(xs, gain, sh, sc, w, *, n_ctx, mode, out_dtype, rope=None, head_gain=None, n_rope_cols=0,
             n_q_cols=0):
    nt, d = xs.shape
    n = w.shape[1]
    prenormed = gain is None
    tm = _pick(nt, (1056, 768, 512, 256) if prenormed else (768, 512, 256))
    tn = _pick(math.gcd(math.gcd(n, n_rope_cols), n_q_cols), (512, 256, 128))
    if prenormed:
        assert xs.dtype == BF16 and sh is None and sc is None
        in_specs = [pl.BlockSpec((tm, d), lambda i, j: (i, 0))]
        args = [xs]
        scratch = []
    else:
        assert n_ctx % ROW_CHUNK == 0
        in_specs = [pl.BlockSpec((tm, d), lambda i, j: (i, 0)),
                    pl.BlockSpec((1, d), lambda i, j: (0, 0)),
                    pl.BlockSpec((2, d), lambda i, j: (0, 0)),
                    pl.BlockSpec((2, d), lambda i, j: (0, 0))]
        args = [xs, gain, sh, sc]
        scratch = [pltpu.VMEM((tm, d), BF16), pltpu.VMEM((tm, 1), F32)]
    in_specs.append(pl.BlockSpec((d, tn), lambda i, j: (0, j)))
    args.append(w)
    if mode in ("rope", "qknorm_rope"):
        assert n_rope_cols % tn == 0
        for t in rope:
            in_specs.append(pl.BlockSpec((tm, HEAD_DIM), lambda i, j: (i, 0)))
            args.append(t)
        if mode == "qknorm_rope":
            in_specs.append(pl.BlockSpec((1, tn), lambda i, j: (0, j)))
            args.append(head_gain)
    return pl.pallas_call(
        functools.partial(_norm_mm_kernel, n_ctx=n_ctx, tm=tm, tn=tn, mode=mode,
                          n_rope_tiles=n_rope_cols // tn, n_q_tiles=n_q_cols // tn, prenormed=prenormed),
        out_shape=jax.ShapeDtypeStruct((nt, n), out_dtype),
        grid=(nt // tm, n // tn),
        in_specs=in_specs,
        out_specs=pl.BlockSpec((tm, tn), lambda i, j: (i, j)),
        scratch_shapes=scratch,
        compiler_params=_params(("arbitrary", "arbitrary")),
        name=("mm_" if prenormed else "norm_mm_") + mode,
    )(*args)


def _resid_mm_kernel(a_ref, w_ref, r_ref, g_ref, o_ref, *, n_ctx, tm):
    i = pl.program_id(0)
    acc = jnp.dot(a_ref[...], w_ref[...], preferred_element_type=F32)
    rows = i * tm + lax.broadcasted_iota(jnp.int32, (tm, 1), 0)
    g = jnp.where(rows < n_ctx, g_ref[1:2, :], g_ref[0:1, :])
    o_ref[...] = r_ref[...] + g * acc


def _resid_mm(a, w, resid, gate, *, n_ctx):
    nt, k = a.shape
    n = w.shape[1]
    tm = _pick(nt, (1056, 768, 640, 512, 256))
    tn = _pick(n, (512, 256, 128))
    return pl.pallas_call(
        functools.partial(_resid_mm_kernel, n_ctx=n_ctx, tm=tm),
        out_shape=jax.ShapeDtypeStruct((nt, n), F32),
        grid=(nt // tm, n // tn),
        in_specs=[pl.BlockSpec((tm, k), lambda i, j: (i, 0)),
                  pl.BlockSpec((k, tn), lambda i, j: (0, j)),
                  pl.BlockSpec((tm, tn), lambda i, j: (i, j)),
                  pl.BlockSpec((2, tn), lambda i, j: (0, j))],
        out_specs=pl.BlockSpec((tm, tn), lambda i, j: (i, j)),
        compiler_params=_params(("arbitrary", "arbitrary")),
        name="resid_mm",
    )(a, w, resid, gate)


QK_FOLD = HEAD_DIM ** -0.5 * math.log2(math.e)
_NT_DIMS = (((1,), (1,)), ((), ()))


def _stack_heads(q_ref, r0, tq):
    return jnp.concatenate([q_ref[r0:r0 + tq, g * HEAD_DIM:(g + 1) * HEAD_DIM] for g in range(GROUP)], axis=0)


def _with_ones(v):
    return jnp.concatenate([v, jnp.ones(v.shape, v.dtype)], axis=1)


def _attn_win_kernel(sink_ref, q_ref, k_ref, v_ref, o_ref, s_buf, *, tq, nq, n_ctx, wk):
    h = pl.program_id(0)
    step = pl.program_id(1)
    nt = k_ref.shape[0]
    n_ctx_tiles = n_ctx // tq

    def window(t):
        tile = step * nq + t
        start = pl.multiple_of(jnp.clip(tile * tq - WINDOW, 0, nt - wk), WINDOW)
        return tile, start

    def s_tile(t):
        _, start = window(t)
        k_all = jnp.concatenate([k_ref[0:n_ctx, :], k_ref[pl.ds(start, wk), :]], axis=0)
        s_buf[t % 2] = lax.dot_general(_stack_heads(q_ref, t * tq, tq), k_all, _NT_DIMS,
                                       preferred_element_type=F32)

    def consume(t):
        tile, start = window(t)
        v_aug = _with_ones(jnp.concatenate([v_ref[0:n_ctx, :], v_ref[pl.ds(start, wk), :]], axis=0))
        c = lax.broadcasted_iota(jnp.int32, (tq, wk), 1)
        r = lax.broadcasted_iota(jnp.int32, (tq, wk), 0)
        rel = (c - r) + ((start - n_ctx) - (tile - n_ctx_tiles) * tq)
        reach = jnp.where(tile >= n_ctx_tiles, WINDOW, -1)
        bias = jnp.where(jnp.abs(rel) <= reach, jnp.where(c >= n_ctx - start, 0.0, NEG), NEG)
        bias = jnp.concatenate([jnp.zeros((tq, n_ctx), F32), bias], axis=1)
        for g in range(GROUP):
            s = s_buf[t % 2, g * tq:(g + 1) * tq, :] + bias
            sk = sink_ref[h * GROUP + g] * math.log2(math.e)
            m = jnp.maximum(jnp.max(s, axis=1, keepdims=True), sk)
            p = jnp.exp2(s - m)
            acc = jnp.dot(p.astype(BF16), v_aug, preferred_element_type=F32)
            o = acc[:, :HEAD_DIM] / (acc[:, HEAD_DIM:] + jnp.exp2(sk - m))
            o_ref[t * tq:(t + 1) * tq, g * HEAD_DIM:(g + 1) * HEAD_DIM] = o.astype(o_ref.dtype)

    s_tile(0)
    for t in range(nq):
        if t + 1 < nq:
            s_tile(t + 1)
        consume(t)


def _attn_win(qkv, sink, *, n_ctx, n_heads, n_kv):
    nt = qkv.shape[0]
    tq = n_ctx
    wk = tq + 2 * WINDOW
    assert tq % WINDOW == 0 and nt % tq == 0 and nt >= wk
    nq = _pick(nt // tq, (3, 2, 1))
    kcol, vcol = n_heads, n_heads + n_kv
    return pl.pallas_call(
        functools.partial(_attn_win_kernel, tq=tq, nq=nq, n_ctx=n_ctx, wk=wk),
        out_shape=jax.ShapeDtypeStruct((nt, n_heads * HEAD_DIM), BF16),
        grid=(n_kv, nt // (nq * tq)),
        in_specs=[pl.BlockSpec(memory_space=pltpu.SMEM),
                  pl.BlockSpec((nq * tq, GROUP * HEAD_DIM), lambda h, i: (i, h)),
                  pl.BlockSpec((nt, HEAD_DIM), lambda h, i: (0, kcol + h)),
                  pl.BlockSpec((nt, HEAD_DIM), lambda h, i: (0, vcol + h))],
        out_specs=pl.BlockSpec((nq * tq, GROUP * HEAD_DIM), lambda h, i: (i, h)),
        scratch_shapes=[pltpu.VMEM((2, GROUP * tq, n_ctx + wk), F32)],
        compiler_params=_params(("arbitrary", "arbitrary")),
        name="attn_win",
    )(sink, qkv, qkv, qkv)


def _attn_full_kernel(q_ref, k_ref, v_ref, o_ref, s_buf, m_scr, acc_scr, *, tq, tk, nq, n_ctx):
    step = pl.program_id(1)
    nt = k_ref.shape[0]
    n_chunks = nt // tk
    n_items = nq * n_chunks

    def q_rows(tile):
        r0 = pl.multiple_of(tile * tq, tq)
        return jnp.concatenate([q_ref[pl.ds(r0, tq), g * HEAD_DIM:(g + 1) * HEAD_DIM] for g in range(GROUP)], axis=0)

    def finish(tile):
        r0 = pl.multiple_of(tile * tq, tq)
        acc = acc_scr[...]
        o = acc[:, :HEAD_DIM] / acc[:, HEAD_DIM:]
        for g in range(GROUP):
            o_ref[pl.ds(r0, tq), g * HEAD_DIM:(g + 1) * HEAD_DIM] = o[g * tq:(g + 1) * tq].astype(o_ref.dtype)

    def s_tile(n, slot):
        tile = lax.div(n, n_chunks)
        off = pl.multiple_of((n - tile * n_chunks) * tk, tk)
        s_buf[slot] = lax.dot_general(q_rows(tile), k_ref[pl.ds(off, tk), :], _NT_DIMS, preferred_element_type=F32)

    def consume(n, slot):
        tile = lax.div(n, n_chunks)
        chunk = n - tile * n_chunks
        off = pl.multiple_of(chunk * tk, tk)
        v_aug = _with_ones(v_ref[pl.ds(off, tk), :])
        for g in range(GROUP):
            rows = slice(g * tq, (g + 1) * tq)
            s = s_buf[slot, rows, :]
            m_prev = jnp.where(chunk == 0, NEG, m_scr[rows, :])
            m_new = jnp.maximum(m_prev, jnp.max(s, axis=1, keepdims=True))
            p = jnp.exp2(s - m_new)
            acc_scr[rows, :] = (jnp.exp2(m_prev - m_new) * acc_scr[rows, :]
                                + jnp.dot(p.astype(BF16), v_aug, preferred_element_type=F32))
            m_scr[rows, :] = m_new
        finish(tile)

    m_scr[...] = jnp.full(m_scr.shape, NEG, F32)

    @pl.when(step == 0)
    def _():
        s = lax.dot_general(q_rows(0), k_ref[0:n_ctx, :], _NT_DIMS, preferred_element_type=F32)
        p = jnp.exp2(s - jnp.max(s, axis=1, keepdims=True))
        acc_scr[...] = jnp.dot(p.astype(BF16), _with_ones(v_ref[0:n_ctx, :]), preferred_element_type=F32)
        finish(0)

    @pl.when(step > 0)
    def _():
        acc_scr[...] = jnp.zeros(acc_scr.shape, F32)

    first = jnp.where(step == 0, n_chunks, 0)
    n_pairs = (n_items - first - 1) // 2
    s_tile(first, 0)

    def pair(j, carry):
        n = first + 2 * j
        s_tile(n + 1, 1)
        consume(n, 0)
        s_tile(n + 2, 0)
        consume(n + 1, 1)
        return carry

    lax.fori_loop(0, n_pairs, pair, 0)
    done = first + 2 * n_pairs

    @pl.when(done == n_items - 1)
    def _():
        consume(n_items - 1, 0)

    @pl.when(done == n_items - 2)
    def _():
        s_tile(n_items - 1, 1)
        consume(n_items - 2, 0)
        consume(n_items - 1, 1)


def _attn_full(qkv, *, n_ctx, n_heads, n_kv):
    nt = qkv.shape[0]
    tq = n_ctx
    tk = _pick(nt, (768, 512, 384, 256, 128))
    n_tiles = nt // tq
    nq = _pick(n_tiles, (11, 3, 2))
    assert nt % tq == 0 and nq >= 2
    kcol, vcol = n_heads, n_heads + n_kv
    return pl.pallas_call(
        functools.partial(_attn_full_kernel, tq=tq, tk=tk, nq=nq, n_ctx=n_ctx),
        out_shape=jax.ShapeDtypeStruct((nt, n_heads * HEAD_DIM), BF16),
        grid=(n_kv, n_tiles // nq),
        in_specs=[pl.BlockSpec((nq * tq, GROUP * HEAD_DIM), lambda h, i: (i, h)),
                  pl.BlockSpec((nt, HEAD_DIM), lambda h, i: (0, kcol + h)),
                  pl.BlockSpec((nt, HEAD_DIM), lambda h, i: (0, vcol + h))],
        out_specs=pl.BlockSpec((nq * tq, GROUP * HEAD_DIM), lambda h, i: (i, h)),
        scratch_shapes=[pltpu.VMEM((2, GROUP * tq, tk), F32), pltpu.VMEM((GROUP * tq, 1), F32),
                        pltpu.VMEM((GROUP * tq, 2 * HEAD_DIM), F32)],
        compiler_params=_params(("arbitrary", "arbitrary")),
        name="attn_full",
    )(qkv, qkv, qkv)


def _rglru_kernel(u_ref, gate_ref, cw_ref, cb_ref, wr_ref, wi_ref, br_ref, bi_ref, lam_ref, o_ref, rec_scr,
                  ga_scr, gb_scr, *, tc, n_chunks, n_ctx_chunks):
    nt, ch = u_ref.shape
    row = lax.broadcasted_iota(jnp.int32, (tc, 1), 0)

    def conv_chunk(c):
        t0 = pl.multiple_of(c * tc, tc)
        x0 = u_ref[pl.ds(t0, tc), :]
        seg_start = (c == 0) | (c == n_ctx_chunks)
        seg_end = (c == n_ctx_chunks - 1) | (c == n_chunks - 1)
        p0 = pl.multiple_of(jnp.maximum(t0 - 8, 0), 8)
        n0 = pl.multiple_of(jnp.minimum(t0 + tc, nt - 8), 8)
        prev = u_ref[pl.ds(p0, 8), :] * jnp.where(seg_start, 0.0, 1.0)
        nxt = u_ref[pl.ds(n0, 8), :] * jnp.where(seg_end, 0.0, 1.0)
        xm1 = jnp.where(row == 0, prev[7:8, :], pltpu.roll(x0, 1, 0))
        xm2 = jnp.where(row == 0, prev[6:7, :], jnp.where(row == 1, prev[7:8, :], pltpu.roll(x0, 2, 0)))
        xp1 = jnp.where(row == tc - 1, nxt[0:1, :], pltpu.roll(x0, tc - 1, 0))
        uc = cw_ref[0:1, :] * xm2 + cw_ref[1:2, :] * xm1 + cw_ref[2:3, :] * x0 + cw_ref[3:4, :] * xp1
        return t0, uc + cb_ref[...]

    def coeffs(uc, d):
        ub = uc.astype(BF16)
        r = jax.nn.sigmoid(jnp.dot(ub, wr_ref[d, 0], preferred_element_type=F32) + br_ref[d:d + 1, :])
        gi = jax.nn.sigmoid(jnp.dot(ub, wi_ref[d, 0], preferred_element_type=F32) + bi_ref[d:d + 1, :])
        x = -lam_ref[d:d + 1, :]
        softplus = jnp.maximum(x, 0.0) + jnp.log1p(jnp.exp(-jnp.abs(x)))
        log_a = (-LRU_C * r) * softplus
        a = jnp.exp(log_a)
        one_minus_a2 = -jnp.tanh(log_a) * (a * a + 1.0)
        return a, jnp.sqrt(one_minus_a2) * (gi * uc)

    def scan_steps(a, b, idx, n, first_step, reverse, axis=0):
        s = first_step
        while s < n:
            if reverse:
                keep = idx < n - s
                a_sh, b_sh = pltpu.roll(a, a.shape[axis] - s, axis), pltpu.roll(b, b.shape[axis] - s, axis)
            else:
                keep = idx >= s
                a_sh, b_sh = pltpu.roll(a, s, axis), pltpu.roll(b, s, axis)
            b = a * jnp.where(keep, b_sh, 0.0) + b
            a = a * jnp.where(keep, a_sh, 1.0)
            s *= 2
        return a, b

    n_groups = tc // SUBLANES
    grow = lax.broadcasted_iota(jnp.int32, (n_groups, 1), 0)

    def scan_chunk(a, b, h0, reverse):
        sub = lax.broadcasted_iota(jnp.int32, (n_groups, SUBLANES, 1), 1)
        a, b = scan_steps(a.reshape(n_groups, SUBLANES, ch), b.reshape(n_groups, SUBLANES, ch), sub, SUBLANES, 1,
                          reverse, axis=1)
        a, b = a.reshape(tc, ch), b.reshape(tc, ch)
        end_row = 0 if reverse else SUBLANES - 1

        def group_ends(scr, v):
            for k in range(ch // LANES):
                scr[k] = v[:, k * LANES:(k + 1) * LANES]
            return jnp.concatenate([scr[k, pl.ds(end_row, n_groups, stride=SUBLANES), :]
                                    for k in range(ch // LANES)], axis=1)

        a_end = group_ends(ga_scr, a)
        b_end = group_ends(gb_scr, b)
        entry = n_groups - 1 if reverse else 0
        b_end = jnp.where(grow == entry, b_end + a_end * h0, b_end)
        _, h_end = scan_steps(a_end, b_end, grow, n_groups, 1, reverse)
        h_in = jnp.where(grow == entry, h0, pltpu.roll(h_end, n_groups - 1 if reverse else 1, 0))
        h_in = jnp.broadcast_to(h_in[:, None, :], (n_groups, SUBLANES, h_in.shape[1])).reshape(tc, h_in.shape[1])
        return a * h_in + b

    def fwd(c, h):
        t0, uc = conv_chunk(c)
        a, b = coeffs(uc, 0)
        hh = scan_chunk(a, b, h, False)
        rec_scr[pl.ds(t0, tc), :] = hh
        return hh[tc - 1:tc, :]

    def bwd(j, h):
        c = jnp.where(j < n_ctx_chunks, n_ctx_chunks - 1 - j, n_chunks - 1 - (j - n_ctx_chunks))
        t0, uc = conv_chunk(c)
        a, b = coeffs(uc, 1)
        hh = scan_chunk(a, b, h, True)
        rec = rec_scr[pl.ds(t0, tc), :] + hh
        o_ref[pl.ds(t0, tc), :] = (gate_ref[pl.ds(t0, tc), :].astype(F32) * rec).astype(o_ref.dtype)
        return hh[0:1, :]

    zero = jnp.zeros((1, ch), F32)
    lax.fori_loop(0, n_chunks, fwd, zero)
    lax.fori_loop(0, n_chunks, bwd, zero)


def _rglru(u, gate, conv_w, conv_b, w_ra, w_ix, b_ra, b_ix, lam, *, n_ctx):
    nt, d = u.shape
    nb, ch = w_ra.shape[1], w_ra.shape[2]
    tc = _pick(n_ctx, (256, 128))
    assert n_ctx % tc == 0 and nt % tc == 0
    col = lambda h: (0, h)
    return pl.pallas_call(
        functools.partial(_rglru_kernel, tc=tc, n_chunks=nt // tc, n_ctx_chunks=n_ctx // tc),
        out_shape=jax.ShapeDtypeStruct((nt, d), BF16),
        grid=(nb,),
        in_specs=[pl.BlockSpec((nt, ch), col),
                  pl.BlockSpec((nt, ch), col),
                  pl.BlockSpec((conv_w.shape[0], ch), col),
                  pl.BlockSpec((1, ch), col),
                  pl.BlockSpec((2, 1, ch, ch), lambda h: (0, h, 0, 0)),
                  pl.BlockSpec((2, 1, ch, ch), lambda h: (0, h, 0, 0)),
                  pl.BlockSpec((2, ch), col),
                  pl.BlockSpec((2, ch), col),
                  pl.BlockSpec((2, ch), col)],
        out_specs=pl.BlockSpec((nt, ch), col),
        scratch_shapes=[pltpu.VMEM((nt, ch), F32), pltpu.VMEM((ch // LANES, tc, LANES), F32),
                        pltpu.VMEM((ch // LANES, tc, LANES), F32)],
        compiler_params=_params(("arbitrary",)),
        name="rglru",
    )(u, gate, conv_w, conv_b, w_ra, w_ix, b_ra, b_ix, lam)


def _ffn_pre_kernel(x_ref, gain_ref, sh_ref, sc_ref, wr_ref, hp_ref, aff_ref, h_scr, rs_scr, *, n_ctx, tm, n_experts):
    i = pl.program_id(0)
    half = x_ref.shape[1] // 2

    def emit(r0, h):
        hb = h.astype(BF16)
        h_scr[pl.ds(r0, ROW_CHUNK), :] = hb
        bits = pltpu.bitcast(hb.astype(F32), jnp.uint32)
        hp_ref[pl.ds(r0, ROW_CHUNK), :] = (bits[:, :half] & jnp.uint32(0xFFFF0000)) | (bits[:, half:] >> 16)

    _modulated_rows(x_ref, gain_ref, sh_ref, sc_ref, rs_scr, i * tm, n_ctx, emit)
    logits = jnp.dot(h_scr[...], wr_ref[...], preferred_element_type=F32)
    lane = lax.broadcasted_iota(jnp.int32, logits.shape, 1)
    logits = jnp.where(lane < n_experts, logits, NEG)
    e = jnp.exp(logits - jnp.max(logits, axis=-1, keepdims=True))
    aff_ref[...] = e / jnp.sum(e, axis=-1, keepdims=True)


def _ffn_pre(xs, gain, sh, sc, w_router_pad, *, n_ctx, n_experts):
    nt, d = xs.shape
    tm = _pick(nt, (768, 512, 256))
    return pl.pallas_call(
        functools.partial(_ffn_pre_kernel, n_ctx=n_ctx, tm=tm, n_experts=n_experts),
        out_shape=(jax.ShapeDtypeStruct((nt, d // 2), jnp.uint32),
                   jax.ShapeDtypeStruct((nt, LANES), F32)),
        grid=(nt // tm,),
        in_specs=[pl.BlockSpec((tm, d), lambda i: (i, 0)),
                  pl.BlockSpec((1, d), lambda i: (0, 0)),
                  pl.BlockSpec((2, d), lambda i: (0, 0)),
                  pl.BlockSpec((2, d), lambda i: (0, 0)),
                  pl.BlockSpec((d, LANES), lambda i: (0, 0))],
        out_specs=(pl.BlockSpec((tm, d // 2), lambda i: (i, 0)),
                   pl.BlockSpec((tm, LANES), lambda i: (i, 0))),
        scratch_shapes=[pltpu.VMEM((tm, d), BF16), pltpu.VMEM((tm, 1), F32)],
        compiler_params=_params(("arbitrary",)),
        name="ffn_pre",
    )(xs, gain, sh, sc, w_router_pad)


def _pack_bf16_pair(hi, lo):
    hi_bits = pltpu.bitcast(hi.astype(BF16).astype(F32), jnp.uint32)
    lo_bits = pltpu.bitcast(lo.astype(BF16).astype(F32), jnp.uint32)
    return (hi_bits & jnp.uint32(0xFFFF0000)) | (lo_bits >> 16)


def _unpack_bf16_pair(w):
    return (pltpu.bitcast(w & jnp.uint32(0xFFFF0000), F32).astype(BF16),
            pltpu.bitcast(w << 16, F32).astype(BF16))


def _moe_kernel(idx_ref, hp_hbm, g_ref, wg_ref, wu_ref, wda_ref, wdb_ref, o_ref, land, xb, act, sem,
                *, nf, tf, per):
    e = pl.program_id(0)
    s = pl.program_id(1)
    n_experts = pl.num_programs(0)
    n_steps = pl.num_programs(1)
    rows, half = xb.shape[0], land.shape[1]

    def gather_row(expert, p):
        src = idx_ref[expert, p]
        pltpu.make_async_copy(hp_hbm.at[pl.ds(src, 1), :], land.at[pl.ds(p, 1), :], sem).start()

    def wait_rows():
        pltpu.make_async_copy(hp_hbm.at[pl.ds(0, land.shape[0]), :], land, sem).wait()

    def gather_ahead():
        nxt = lax.rem(e + 1, n_experts)
        for k in range(per):
            gather_row(nxt, s * per + k)

    @pl.when((e == 0) & (s == 0))
    def _():
        def body(p, carry):
            gather_row(0, p)
            return carry

        lax.fori_loop(0, land.shape[0], body, 0)

    @pl.when(s == 0)
    def _():
        wait_rows()
        xb[:, :half], xb[:, half:] = _unpack_bf16_pair(land[0:rows, :])

    @pl.when(s < nf)
    def _():
        gather_ahead()
        x = xb[...]
        a = jnp.dot(x, wg_ref[0, 0].astype(BF16), preferred_element_type=F32)
        u = jnp.dot(x, wu_ref[0, 0].astype(BF16), preferred_element_type=F32)
        act[s] = (a * jax.nn.sigmoid(a) * u).astype(BF16)

    @pl.when(s >= nf)
    def _():
        gather_ahead()

        def down(wd_ref):
            acc = jnp.dot(act[0], wd_ref[0, 0, 0:tf, :].astype(BF16), preferred_element_type=F32)
            for f in range(1, nf):
                acc = acc + jnp.dot(act[f], wd_ref[0, 0, f * tf:(f + 1) * tf, :].astype(BF16),
                                    preferred_element_type=F32)
            return acc * g_ref[0]

        o_ref[0] = _pack_bf16_pair(down(wda_ref), down(wdb_ref))

    @pl.when((e == n_experts - 1) & (s == n_steps - 1))
    def _():
        wait_rows()


def _moe(idx, hp, gates, w_gate, w_up, w_down, layer):
    n_experts, rows = idx.shape
    half = hp.shape[1]
    dexp = w_gate.shape[3]
    tf = _pick(dexp, (256, 128))
    tp = _pick(half, (512, 256, 128))
    nf, nd = dexp // tf, half // tp
    per = -(-rows // ((nf + nd) * SUBLANES)) * SUBLANES
    rows_pad = per * (nf + nd)
    idx = jnp.concatenate([idx, jnp.broadcast_to(idx[:, -1:], (n_experts, rows_pad - rows))], axis=1)
    up_map = lambda e, s, idx: (layer, e, 0, jnp.minimum(s, nf - 1))
    grid_spec = pltpu.PrefetchScalarGridSpec(
        num_scalar_prefetch=1,
        grid=(n_experts, nf + nd),
        in_specs=[pl.BlockSpec(memory_space=pl.ANY),
                  pl.BlockSpec((1, rows, 1), lambda e, s, idx: (e, 0, 0)),
                  pl.BlockSpec((1, 1, 2 * half, tf), up_map),
                  pl.BlockSpec((1, 1, 2 * half, tf), up_map),
                  pl.BlockSpec((1, 1, dexp, tp), lambda e, s, idx: (layer, e, 0, jnp.maximum(s - nf, 0))),
                  pl.BlockSpec((1, 1, dexp, tp), lambda e, s, idx: (layer, e, 0, jnp.maximum(s - nf, 0) + nd))],
        out_specs=pl.BlockSpec((1, rows, tp), lambda e, s, idx: (e, 0, jnp.maximum(s - nf, 0))),
        scratch_shapes=[pltpu.VMEM((rows_pad, half), jnp.uint32),
                        pltpu.VMEM((rows, 2 * half), BF16),
                        pltpu.VMEM((nf, rows, tf), BF16),
                        pltpu.SemaphoreType.DMA(())],
    )
    return pl.pallas_call(
        functools.partial(_moe_kernel, nf=nf, tf=tf, per=per),
        out_shape=jax.ShapeDtypeStruct((n_experts, rows, half), jnp.uint32),
        grid_spec=grid_spec,
        compiler_params=_params(("arbitrary", "arbitrary")),
        name="moe_experts",
    )(idx, hp, gates, w_gate, w_up, w_down, w_down)


COMBINE_TB = 256
COMBINE_W = 64


def _combine_kernel(src_ref, nwin_ref, y_hbm, slots_ref, pos_ref, rep_ref, x_ref, g_ref, *rest,
                    n_experts, rows, n_ctx, final):
    if final:
        final_gain_ref, o_ref, ybuf, yextra, sem, sem_extra = rest
    else:
        final_gain_ref = None
        next_gain_ref, next_sh_ref, next_sc_ref, o_ref, h_ref, ybuf, yextra, sem, sem_extra = rest
    b = pl.program_id(0)
    n_blocks = pl.num_programs(0)
    tb, d = x_ref.shape
    half = d // 2
    n_windows = slots_ref.shape[1]
    w = COMBINE_W
    slot = lax.rem(b, 2)

    q = pos_ref[...] + 1
    q_hi = (q >> 5).astype(F32).astype(BF16)
    q_lo = (q & 31).astype(F32).astype(BF16)
    pos_wide = (32.0 * jnp.dot(q_hi, rep_ref[...], preferred_element_type=F32)
                + jnp.dot(q_lo, rep_ref[...], preferred_element_type=F32) - 1.0)

    def fetch(block, window, dst, dma_sem):
        for e in range(n_experts):
            src = pl.multiple_of(e * rows + src_ref[(block * n_windows + window) * n_experts + e], SUBLANES)
            pltpu.make_async_copy(y_hbm.at[pl.ds(src, w), :], dst.at[pl.ds(e * w, w), :], dma_sem).start()

    def wait(dst, dma_sem):
        pltpu.make_async_copy(y_hbm.at[pl.ds(0, n_experts * w), :], dst, dma_sem).wait()

    def expand(buf, window, first):
        window_slots = slots_ref[0, pl.ds(window, 1), :].astype(F32)
        onehot = jnp.where(pos_wide == window_slots, 1.0, 0.0).astype(BF16)
        chunk = _pick(half, (512, 256, 128))
        for k in range(half // chunk):
            cols = slice(k * chunk, (k + 1) * chunk)
            cols_hi = slice(half + k * chunk, half + (k + 1) * chunk)
            hi, lo = _unpack_bf16_pair(buf[:, cols])
            a_hi = jnp.dot(onehot, hi, preferred_element_type=F32)
            a_lo = jnp.dot(onehot, lo, preferred_element_type=F32)
            if first:
                o_ref[:, cols] = a_hi
                o_ref[:, cols_hi] = a_lo
            else:
                o_ref[:, cols] += a_hi
                o_ref[:, cols_hi] += a_lo

    @pl.when(b == 0)
    def _():
        fetch(0, 0, ybuf.at[0], sem.at[0])

    wait(ybuf.at[slot], sem.at[slot])

    @pl.when(b + 1 < n_blocks)
    def _():
        fetch(b + 1, 0, ybuf.at[1 - slot], sem.at[1 - slot])

    expand(ybuf.at[slot], 0, True)

    def extra(window, carry):
        fetch(b, window, yextra, sem_extra)
        wait(yextra, sem_extra)
        expand(yextra, window, False)
        return carry

    lax.fori_loop(1, nwin_ref[b], extra, 0)

    row = b * tb + lax.broadcasted_iota(jnp.int32, (tb, 1), 0)
    g = jnp.where(row < n_ctx, g_ref[1:2, :], g_ref[0:1, :])
    x_new = x_ref[...] + g * o_ref[...]
    if final_gain_ref is None:
        o_ref[...] = x_new
        is_ctx = b * tb < n_ctx
        sc = jnp.where(is_ctx, next_sc_ref[1:2, :], next_sc_ref[0:1, :])
        sh = jnp.where(is_ctx, next_sh_ref[1:2, :], next_sh_ref[0:1, :])

        def modulate_rows(r, carry):
            r0 = pl.multiple_of(r * ROW_CHUNK, ROW_CHUNK)
            x = o_ref[pl.ds(r0, ROW_CHUNK), :]
            ms = jnp.mean(x * x, axis=-1, keepdims=True)
            y = (x * lax.rsqrt(ms + EPS)) * next_gain_ref[...]
            h_ref[pl.ds(r0, ROW_CHUNK), :] = (y * (1.0 + sc) + sh).astype(BF16)
            return carry

        lax.fori_loop(0, tb // ROW_CHUNK, modulate_rows, 0)
    else:
        ms = jnp.mean(x_new * x_new, axis=-1, keepdims=True)
        o_ref[...] = (x_new * lax.rsqrt(ms + EPS)) * final_gain_ref[...]


def _combine_plan(pos, r):
    nt, n_experts = pos.shape
    tb, w = COMBINE_TB, COMBINE_W
    served = w - SUBLANES
    nb, nw = nt // tb, -(-tb // served)
    per_block = jnp.sum((pos >= 0).astype(jnp.int32).reshape(nb, tb, n_experts), axis=1)
    stop = jnp.cumsum(per_block, axis=0)
    start = stop - per_block
    pos = jnp.pad(pos, ((0, 0), (0, LANES - n_experts)), constant_values=-1)
    first = start[:, None, :] + jnp.arange(nw, dtype=jnp.int32)[None, :, None] * served
    src = jnp.minimum(first // SUBLANES * SUBLANES, r - w)
    slot = src[..., None] + jnp.arange(w, dtype=jnp.int32)
    last = jnp.minimum(first + served, stop[:, None, :])
    valid = (slot >= first[..., None]) & (slot < last[..., None])
    slots = jnp.where(valid, slot, -2).reshape(nb, nw, n_experts * w)
    n_win = jnp.maximum(1, jnp.max((stop - start + served - 1) // served, axis=1)).astype(jnp.int32)
    lane_expert = jnp.arange(n_experts * w, dtype=jnp.int32) // w
    rep = (jnp.arange(LANES, dtype=jnp.int32)[:, None] == lane_expert[None, :]).astype(BF16)
    return src.reshape(-1), n_win, slots, pos, rep


def _combine(y_packed, pos, xs, gate, *, n_ctx, next_mod=None, final_gain=None):
    assert (next_mod is None) != (final_gain is None)
    nt, d = xs.shape
    n_experts, r, half = y_packed.shape
    tb = COMBINE_TB
    assert nt % tb == 0 and r >= COMBINE_W and r % SUBLANES == 0 and half % LANES == 0
    src, n_win, slots, pos, rep = _combine_plan(pos, r)
    nw = slots.shape[1]
    final = final_gain is not None
    in_specs = [pl.BlockSpec(memory_space=pl.ANY),
                pl.BlockSpec((1, nw, n_experts * COMBINE_W), lambda b, src, nwin: (b, 0, 0)),
                pl.BlockSpec((tb, LANES), lambda b, src, nwin: (b, 0)),
                pl.BlockSpec((LANES, n_experts * COMBINE_W), lambda b, src, nwin: (0, 0)),
                pl.BlockSpec((tb, d), lambda b, src, nwin: (b, 0)),
                pl.BlockSpec((2, d), lambda b, src, nwin: (0, 0))]
    args = [src, n_win, y_packed.reshape(n_experts * r, half), slots, pos, rep, xs, gate]
    assert n_ctx % tb == 0 and tb % ROW_CHUNK == 0
    row_map = lambda b, src, nwin: (b, 0)
    if final:
        in_specs.append(pl.BlockSpec((1, d), lambda b, src, nwin: (0, 0)))
        args.append(final_gain)
        out_shape = jax.ShapeDtypeStruct((nt - n_ctx, d), F32)
        out_specs = pl.BlockSpec((tb, d), lambda b, src, nwin: (jnp.maximum(b - n_ctx // tb, 0), 0))
    else:
        next_gain, next_sh, next_sc = next_mod
        in_specs += [pl.BlockSpec((1, d), lambda b, src, nwin: (0, 0)),
                     pl.BlockSpec((2, d), lambda b, src, nwin: (0, 0)),
                     pl.BlockSpec((2, d), lambda b, src, nwin: (0, 0))]
        args += [next_gain, next_sh, next_sc]
        out_shape = (jax.ShapeDtypeStruct((nt, d), F32), jax.ShapeDtypeStruct((nt, d), BF16))
        out_specs = (pl.BlockSpec((tb, d), row_map), pl.BlockSpec((tb, d), row_map))
    grid_spec = pltpu.PrefetchScalarGridSpec(
        num_scalar_prefetch=2,
        grid=(nt // tb,),
        in_specs=in_specs,
        out_specs=out_specs,
        scratch_shapes=[pltpu.VMEM((2, n_experts * COMBINE_W, half), jnp.uint32),
                        pltpu.VMEM((n_experts * COMBINE_W, half), jnp.uint32),
                        pltpu.SemaphoreType.DMA((2,)),
                        pltpu.SemaphoreType.DMA(())],
    )
    return pl.pallas_call(
        functools.partial(_combine_kernel, n_experts=n_experts, rows=r, n_ctx=n_ctx, final=final),
        out_shape=out_shape,
        grid_spec=grid_spec,
        compiler_params=_params(("arbitrary",)),
        name="moe_combine",
    )(*args)


def _rope_tables(seq, n_ctx):
    rows = seq // GRID_W
    row = jnp.repeat(jnp.arange(rows), GRID_W).astype(F32)
    col = jnp.tile(jnp.arange(GRID_W), rows).astype(F32)
    inv_freq = ROPE_THETA ** (-jnp.arange(ROPE_FREQS, dtype=F32) / ROPE_FREQS)
    ang_r = row[:, None] * inv_freq
    ang_c = col[:, None] * inv_freq
    zero = jnp.zeros_like(ang_r)
    cos = jnp.concatenate([jnp.cos(ang_r), jnp.cos(ang_r), jnp.cos(ang_c), jnp.cos(ang_c)], axis=1)
    s1 = jnp.concatenate([-jnp.sin(ang_r), zero, -jnp.sin(ang_c), zero], axis=1)
    s2 = jnp.concatenate([zero, jnp.sin(ang_r), zero, jnp.sin(ang_c)], axis=1)
    pad = lambda t, v: jnp.concatenate([jnp.full((n_ctx, HEAD_DIM), v, F32), t], axis=0)
    return pad(cos, 1.0), pad(s1, 0.0), pad(s2, 0.0)


def _route(aff, n_ctx, n_experts):
    nt = aff.shape[0]
    a_ctx = aff[:n_ctx, :n_experts]
    a_lat = aff[n_ctx:, :n_experts]
    cap_l = CAPACITY_FACTOR * (nt - n_ctx) // n_experts
    cap_c = CAPACITY_FACTOR * n_ctx // n_experts
    g_l, i_l = lax.top_k(a_lat.T, cap_l)
    g_c, i_c = lax.top_k(a_ctx.T, cap_c)
    idx = jnp.concatenate([i_l + n_ctx, i_c], axis=1).astype(jnp.int32)
    gates = jnp.concatenate([g_l, g_c], axis=1)
    idx, gates = lax.sort((idx, gates), dimension=1, num_keys=1)

    def slots(a, kth, cap, base):
        above, equal = a > kth[None, :], a == kth[None, :]
        room = cap - jnp.sum(above, axis=0, dtype=jnp.int32)
        picked = above | (equal & (jnp.cumsum(equal, axis=0, dtype=jnp.int32) <= room[None, :]))
        return jnp.where(picked, base + jnp.cumsum(picked, axis=0, dtype=jnp.int32) - 1, -1)

    pos = jnp.concatenate([slots(a_ctx, g_c[:, -1], cap_c, 0), slots(a_lat, g_l[:, -1], cap_l, cap_c)], axis=0)
    return idx, gates, pos


def kernel(x, c, ctx, c_ctx, mod_down, mod_up, mod_bias, norm_mix, norm_ffn, norm_final, a_w_qkv, a_w_o, a_sink, b_w_qkv, b_w_o, b_q_norm, b_k_norm, c_w_in, c_conv_w, c_conv_b, c_w_ra, c_b_ra, c_w_ix, c_b_ix, c_lambda, c_w_out, moe_router, moe_w_gate, moe_w_up, moe_w_down):
    assert x.shape[0] == 1 and ctx.shape[0] == 1
    seq, d = x.shape[1], x.shape[2]
    n_ctx = ctx.shape[1]
    depth = mod_down.shape[0]
    n_experts = moe_router.shape[2]
    n_heads = a_w_o.shape[1] // HEAD_DIM
    n_kv = n_heads // GROUP
    d_rnn = c_w_out.shape[1]

    xs = jnp.concatenate([ctx[0], x[0]], axis=0)
    cond = jnp.concatenate([c, c_ctx[None, :], jnp.zeros((6, d), F32)], axis=0)
    rope = _rope_tables(seq, n_ctx)
    zero_bias = jnp.zeros((1, mod_down.shape[2]), F32)

    mods = []
    for i in range(depth):
        low = _small_mm(cond, mod_down[i], zero_bias, silu=True)
        mod = _small_mm(low, mod_up[i], mod_bias[i][None, :], silu=False)[:2].reshape(2, 6, d)
        mods.append([mod[:, k, :] for k in range(6)])

    h = None
    for i in range(depth):
        sh1, sc1, g1, sh2, sc2, g2 = mods[i]
        src, pre = (xs, (norm_mix[i][None, :], sh1, sc1)) if h is None else (h, (None, None, None))
        kind, j = i % 3, i // 3
        if kind == 0:
            qkv = _norm_mm(src, *pre, a_w_qkv[j].astype(BF16), n_ctx=n_ctx, mode="rope",
                           out_dtype=BF16, rope=rope, n_rope_cols=(n_heads + n_kv) * HEAD_DIM,
                           n_q_cols=n_heads * HEAD_DIM)
            y = _attn_win(qkv, a_sink[j], n_ctx=n_ctx, n_heads=n_heads, n_kv=n_kv)
            w_out = a_w_o[j]
        elif kind == 1:
            head_gain = jnp.concatenate([jnp.tile(b_q_norm[j], n_heads), jnp.tile(b_k_norm[j], n_kv),
                                         jnp.ones((n_kv * HEAD_DIM,), F32)])[None, :]
            qkv = _norm_mm(src, *pre, b_w_qkv[j].astype(BF16), n_ctx=n_ctx, mode="qknorm_rope",
                           out_dtype=BF16, rope=rope, head_gain=head_gain,
                           n_rope_cols=(n_heads + n_kv) * HEAD_DIM, n_q_cols=n_heads * HEAD_DIM)
            y = _attn_full(qkv, n_ctx=n_ctx, n_heads=n_heads, n_kv=n_kv)
            w_out = b_w_o[j]
        else:
            gate = _norm_mm(src, *pre, c_w_in[j][:, :d_rnn].astype(BF16), n_ctx=n_ctx,
                            mode="gelu", out_dtype=BF16)
            u = _norm_mm(src, *pre, c_w_in[j][:, d_rnn:].astype(BF16), n_ctx=n_ctx,
                         mode="none", out_dtype=F32)
            y = _rglru(u, gate, c_conv_w[j], c_conv_b[j][None, :], c_w_ra[j].astype(BF16),
                       c_w_ix[j].astype(BF16), c_b_ra[j], c_b_ix[j], c_lambda[j], n_ctx=n_ctx)
            w_out = c_w_out[j]
        xs = _resid_mm(y, w_out.astype(BF16), xs, g1, n_ctx=n_ctx)

        w_router = jnp.pad(moe_router[i], ((0, 0), (0, LANES - n_experts))).astype(BF16)
        hp, aff = _ffn_pre(xs, norm_ffn[i][None, :], sh2, sc2, w_router, n_ctx=n_ctx, n_experts=n_experts)
        idx, gates, pos = _route(aff, n_ctx, n_experts)
        yexp = _moe(idx, hp, gates[:, :, None], moe_w_gate, moe_w_up, moe_w_down, i)
        if i == depth - 1:
            return _combine(yexp, pos, xs, g2, n_ctx=n_ctx, final_gain=norm_final[None, :])[None]
        xs, h = _combine(yexp, pos, xs, g2, n_ctx=n_ctx,
                         next_mod=(norm_mix[i + 1][None, :], mods[i + 1][0], mods[i + 1][1]))
```
